```python
import jax, jax.numpy as jnp
from jax import lax
import numpy as np


D_MODEL = 2048
BATCH = 1
SEQ = 8192
DEPTH = 4

N_A_LAYERS = DEPTH // 2
N_B_LAYERS = DEPTH - N_A_LAYERS
RWKV_HEAD = 64
RWKV_HEADS = D_MODEL // RWKV_HEAD
LORA_DECAY = max(32, int(round(1.8 * D_MODEL ** 0.5 / 32)) * 32)
LORA_ICLR = max(32, int(round(1.8 * D_MODEL ** 0.5 / 32)) * 32)
LORA_VRES = max(32, int(round(1.3 * D_MODEL ** 0.5 / 32)) * 32)
LORA_GATE = max(32, int(round(0.6 * D_MODEL ** 0.8 / 32)) * 32)
GN_EPS = 64e-5
FOX_HEAD = 128
FOX_HEADS = D_MODEL // FOX_HEAD
Q_BLOCK = 128
N_EXPERTS = 16
N_GROUPS = 4
EXPERTS_PER_GROUP = N_EXPERTS // N_GROUPS
TOP_K = 2
D_EXPERT = D_MODEL // 2
ALPHA = (2 * DEPTH) ** 0.25
BETA = (8 * DEPTH) ** -0.25
LN_EPS = 1e-5
RMS_EPS = 1e-6

kernel_name = 'hybrid_rwkv7_fox_yoco_grouped_moe'


def layer_norm(t, g, b):
    t32 = t.astype(jnp.float32)
    mu = t32.mean(-1, keepdims=True)
    var = ((t32 - mu) ** 2).mean(-1, keepdims=True)
    return ((t32 - mu) * lax.rsqrt(var + LN_EPS) * g + b).astype(t.dtype)


def rms_heads(t, gain):
    t32 = t.astype(jnp.float32)
    return (t32 * lax.rsqrt(jnp.mean(t32 * t32, -1, keepdims=True) + RMS_EPS) * gain).astype(t.dtype)


def rwkv7_scan(r, decay, k, v, a, b):
    B, T, H, N = r.shape
    tm = lambda t: jnp.moveaxis(t, 1, 0)

    def step(S, inp):
        r_t, w_t, k_t, v_t, a_t, b_t = inp
        sa = jnp.einsum('bhvk,bhk->bhv', S, a_t)
        S = S * w_t[:, :, None, :] + sa[..., None] * b_t[:, :, None, :] + v_t[..., None] * k_t[:, :, None, :]
        return S, jnp.einsum('bhvk,bhk->bhv', S, r_t)

    S0 = jnp.zeros((B, H, N, N), jnp.float32)
    _, y = lax.scan(step, S0, (tm(r), tm(decay), tm(k), tm(v), tm(a), tm(b)))
    return jnp.moveaxis(y, 0, 1)


def rwkv7_time_mix(x, v_first, vres, mu, w_rkv, w0, w_a, w_b, a0, a_a, a_b, g_a, g_b,
                   k_k, k_a, r_k, lnx_g, lnx_b, w_o):
    B, T, D = x.shape
    H, N = RWKV_HEADS, RWKV_HEAD
    x_prev = jnp.pad(x, ((0, 0), (1, 0), (0, 0)))[:, :T]
    xs = x[None] + (x_prev - x)[None] * mu[:, None, None, :]
    rkv = jnp.einsum('cbtd,cde->cbte', xs[:3], w_rkv)
    r, k, v = rkv[0], rkv[1], rkv[2]
    w_raw = (w0 + jnp.tanh(xs[3] @ w_a) @ w_b).astype(jnp.float32)
    decay = jnp.exp(-jnp.exp(-jax.nn.softplus(-w_raw) - 0.5))
    a = jax.nn.sigmoid(a0 + (xs[4] @ a_a) @ a_b)
    g = jax.nn.sigmoid(xs[5] @ g_a) @ g_b
    if vres is None:
        v_first = v
    else:
        v0, v_a, v_b = vres
        v = v + (v_first - v) * jax.nn.sigmoid(v0 + (xs[2] @ v_a) @ v_b)
    heads = lambda t: t.reshape(B, T, H, N).astype(jnp.float32)
    kk = heads(k * k_k)
    kk = kk / jnp.maximum(jnp.linalg.norm(kk, axis=-1, keepdims=True), 1e-12)
    k = k * (1 + (a - 1) * k_a)
    rh, kh, vh, ah = heads(r), heads(k), heads(v), heads(a)
    y = rwkv7_scan(rh, heads(decay), kh, vh, -kk, kk * ah)
    mean = y.mean(-1, keepdims=True)
    var = ((y - mean) ** 2).mean(-1, keepdims=True)
    y = ((y - mean) * lax.rsqrt(var + GN_EPS)).reshape(B, T, D) * lnx_g + lnx_b
    y = y + (jnp.sum(rh * kh * r_k, -1, keepdims=True) * vh).reshape(B, T, D)
    out = (y.astype(x.dtype) * g) @ w_o
    return out, v_first


def fox_shared_kv(x, w_kvf, b_f, k_norm):
    B, T, D = x.shape
    H, Dh = FOX_HEADS, FOX_HEAD
    proj = x @ w_kvf
    k = rms_heads(proj[..., :D].reshape(B, T, H, Dh), k_norm)
    v = proj[..., D:2 * D].reshape(B, T, H, Dh)
    log_f = jax.nn.log_sigmoid((proj[..., 2 * D:] + b_f).astype(jnp.float32))
    c = jnp.cumsum(log_f, axis=1)
    return k.transpose(0, 2, 1, 3), v.transpose(0, 2, 1, 3), c.transpose(0, 2, 1)


def forgetting_attention(x, k, v, c, w_qg, q_norm, w_o):
    B, T, D = x.shape
    H, Dh = FOX_HEADS, FOX_HEAD
    qg = x @ w_qg
    q = rms_heads(qg[..., :D].reshape(B, T, H, Dh), q_norm).transpose(0, 2, 1, 3)
    gate = jax.nn.sigmoid(qg[..., D:])
    n_blk = T // Q_BLOCK
    q_blocks = q.reshape(B, H, n_blk, Q_BLOCK, Dh).transpose(2, 0, 1, 3, 4)
    c_blocks = c.reshape(B, H, n_blk, Q_BLOCK).transpose(2, 0, 1, 3)
    key_pos = jnp.arange(T)
    scale = Dh ** -0.5

    def one_block(args):
        i, q_i, c_i = args
        q_pos = i * Q_BLOCK + jnp.arange(Q_BLOCK)
        s = jnp.einsum('bhqd,bhkd->bhqk', q_i, k, preferred_element_type=jnp.float32) * scale
        s = s + (c_i[..., :, None] - c[..., None, :])
        s = jnp.where(key_pos[None, :] <= q_pos[:, None], s, -jnp.inf)
        p = jax.nn.softmax(s, axis=-1).astype(v.dtype)
        return jnp.einsum('bhqk,bhkd->bhqd', p, v)

    o = lax.map(one_block, (jnp.arange(n_blk), q_blocks, c_blocks))
    o = o.transpose(1, 0, 3, 2, 4).reshape(B, T, D)
    return (o * gate) @ w_o


def grouped_moe(x, router_w, router_b, w_in, w_out):
    B, T, D = x.shape
    F = D_EXPERT
    xt = x.reshape(B * T, D)
    scores = jax.nn.sigmoid((xt @ router_w).astype(jnp.float32))
    biased = scores + router_b.astype(jnp.float32)
    grouped = biased.reshape(-1, N_GROUPS, EXPERTS_PER_GROUP)
    group_score = lax.top_k(grouped, TOP_K)[0].sum(-1)
    g_sel = jnp.argmax(group_score, axis=-1)
    in_group = jnp.take_along_axis(grouped, g_sel[:, None, None], axis=1)[:, 0]
    _, local = lax.top_k(in_group, TOP_K)
    expert_idx = g_sel[:, None] * EXPERTS_PER_GROUP + local
    w_sel = jnp.take_along_axis(scores, expert_idx, axis=-1)
    w_sel = w_sel / jnp.sum(w_sel, -1, keepdims=True)
    combine = jnp.einsum('nk,nke->ne', w_sel, jax.nn.one_hot(expert_idx, N_EXPERTS, dtype=jnp.float32))
    out = jnp.zeros((B * T, D), jnp.float32)
    for e in range(N_EXPERTS):
        h = xt @ w_in[e]
        y_e = (h[:, :F] * jax.nn.silu(h[:, F:])) @ w_out[e]
        out = out + combine[:, e:e + 1] * y_e.astype(jnp.float32)
    return out.astype(x.dtype).reshape(B, T, D)


def setup_inputs(seed: int = 0) -> dict:
    key = jax.random.key(seed)
    ks = iter(jax.random.split(key, 48))
    nrm = lambda shape, scale: scale * jax.random.normal(next(ks), shape, jnp.float32)
    D, NA, NB, H = D_MODEL, N_A_LAYERS, N_B_LAYERS, FOX_HEADS
    inv = D ** -0.5
    ratio = jnp.arange(D, dtype=jnp.float32) / (D - 1)
    return {
        'x': nrm((BATCH, SEQ, D), 1.0),
        'rw_mu': jax.random.uniform(next(ks), (NA, 6, D), jnp.float32),
        'rw_w_rkv': jnp.concatenate([nrm((NA, 2, D, D), inv), nrm((NA, 1, D, D), BETA * inv)], axis=1),
        'rw_w0': (-6.0 + 5.0 * ratio ** 0.85 + 0.5)[None] + nrm((NA, D), 0.1),
        'rw_w_a': nrm((NA, D, LORA_DECAY), inv),
        'rw_w_b': nrm((NA, LORA_DECAY, D), 0.1 * LORA_DECAY ** -0.5),
        'rw_a0': nrm((NA, D), 0.1),
        'rw_a_a': nrm((NA, D, LORA_ICLR), inv),
        'rw_a_b': nrm((NA, LORA_ICLR, D), 0.1 * LORA_ICLR ** -0.5),
        'rw_v0': 1.0 + nrm((NA - 1, D), 0.1),
        'rw_v_a': nrm((NA - 1, D, LORA_VRES), inv),
        'rw_v_b': nrm((NA - 1, LORA_VRES, D), 0.1 * LORA_VRES ** -0.5),
        'rw_g_a': nrm((NA, D, LORA_GATE), inv),
        'rw_g_b': nrm((NA, LORA_GATE, D), LORA_GATE ** -0.5),
        'rw_k_k': 0.85 + nrm((NA, D), 0.02),
        'rw_k_a': 1.0 + nrm((NA, D), 0.02),
        'rw_r_k': nrm((NA, RWKV_HEADS, RWKV_HEAD), 0.1),
        'rw_lnx_g': 1.0 + nrm((NA, D), 0.02),
        'rw_lnx_b': nrm((NA, D), 0.02),
        'rw_w_o': nrm((NA, D, D), BETA * inv),
        'fx_w_kvf': jnp.concatenate([nrm((D, D), inv), nrm((D, D), BETA * inv), nrm((D, H), inv)], axis=1),
        'fx_b_f': jnp.linspace(1.0, 6.0, H, dtype=jnp.float32) + nrm((H,), 0.1),
        'fx_k_norm': 1.0 + nrm((FOX_HEAD,), 0.02),
        'fx_w_qg': nrm((NB, D, 2 * D), inv),
        'fx_q_norm': 1.0 + nrm((NB, FOX_HEAD), 0.02),
        'fx_w_o': nrm((NB, D, D), BETA * inv),
        'router_w': nrm((D, N_EXPERTS), inv),
        'router_b': nrm((N_EXPERTS,), 0.01),
        'moe_w_in': nrm((DEPTH, N_EXPERTS, D, 2 * D_EXPERT), inv),
        'moe_w_out': nrm((DEPTH, N_EXPERTS, D_EXPERT, D), BETA * D_EXPERT ** -0.5),
        'ln_g': 1.0 + nrm((DEPTH, 2, D), 0.02),
        'ln_b': nrm((DEPTH, 2, D), 0.02),
    }


def reference(x, rw_mu, rw_w_rkv, rw_w0, rw_w_a, rw_w_b, rw_a0, rw_a_a, rw_a_b, rw_v0, rw_v_a, rw_v_b,
              rw_g_a, rw_g_b, rw_k_k, rw_k_a, rw_r_k, rw_lnx_g, rw_lnx_b, rw_w_o,
              fx_w_kvf, fx_b_f, fx_k_norm, fx_w_qg, fx_q_norm, fx_w_o,
              router_w, router_b, moe_w_in, moe_w_out, ln_g, ln_b):
    v_first = None
    k_sh = v_sh = c_sh = None
    for l in range(DEPTH):
        if l < N_A_LAYERS:
            vres = None if l == 0 else (rw_v0[l - 1], rw_v_a[l - 1], rw_v_b[l - 1])
            h, v_first = rwkv7_time_mix(x, v_first, vres, rw_mu[l], rw_w_rkv[l], rw_w0[l], rw_w_a[l], rw_w_b[l],
                                        rw_a0[l], rw_a_a[l], rw_a_b[l], rw_g_a[l], rw_g_b[l], rw_k_k[l],
                                        rw_k_a[l], rw_r_k[l], rw_lnx_g[l], rw_lnx_b[l], rw_w_o[l])
        else:
            if l == N_A_LAYERS:
                k_sh, v_sh, c_sh = fox_shared_kv(x, fx_w_kvf, fx_b_f, fx_k_norm)
            j = l - N_A_LAYERS
            h = forgetting_attention(x, k_sh, v_sh, c_sh, fx_w_qg[j], fx_q_norm[j], fx_w_o[j])
        x = layer_norm(ALPHA * x + h, ln_g[l, 0], ln_b[l, 0])
        x = layer_norm(ALPHA * x + grouped_moe(x, router_w, router_b, moe_w_in[l], moe_w_out[l]),
                       ln_g[l, 1], ln_b[l, 1])
    return x
```

```python
import functools

import jax
import jax.numpy as jnp
from jax import lax
from jax.experimental import pallas as pl
from jax.experimental.pallas import tpu as pltpu

F32 = jnp.float32
BF16 = jnp.bfloat16

RWKV_HEAD = 64
FOX_HEAD = 128
N_EXPERTS = 16
N_GROUPS = 4
EXPERTS_PER_GROUP = N_EXPERTS // N_GROUPS
GN_EPS = 64e-5
LN_EPS = 1e-5
RMS_EPS = 1e-6
CHUNK = 64
LANES = 128
VMEM_LIMIT = 56 * 1024 * 1024


def _cparams(n_axes, vmem=VMEM_LIMIT):
    return pltpu.CompilerParams(dimension_semantics=("arbitrary",) * n_axes, vmem_limit_bytes=vmem)


def _dot(a, b):
    return jnp.dot(a, b, preferred_element_type=F32)


def _dot_nt(a, b):
    return lax.dot_general(a, b, (((1,), (1,)), ((), ())), preferred_element_type=F32)


def _split(a):
    hi = a.astype(BF16)
    lo = (a - hi.astype(F32)).astype(BF16)
    return hi, lo


def _dot3(a, b):
    ah, al = _split(a)
    bh, bl = _split(b)
    return _dot(ah, bh) + _dot(ah, bl) + _dot(al, bh)


def _layer_norm(z, g, b):
    mu = jnp.mean(z, axis=-1, keepdims=True)
    zc = z - mu
    var = jnp.mean(zc * zc, axis=-1, keepdims=True)
    return zc * lax.rsqrt(var + LN_EPS) * g + b


def _rowwise(body, row_args, const_args, outs, *, tm, name, scratch=()):
    T = row_args[0].shape[0]
    tm = min(tm, T)
    assert T % tm == 0
    in_specs = [pl.BlockSpec((tm, a.shape[1]), lambda i: (i, 0)) for a in row_args]
    in_specs += [pl.BlockSpec(a.shape, lambda i, nd=a.ndim: (0,) * nd) for a in const_args]
    out_shape = [jax.ShapeDtypeStruct((T, w), dt) for (w, dt) in outs]
    out_specs = [pl.BlockSpec((tm, w), lambda i: (i, 0)) for (w, dt) in outs]
    res = pl.pallas_call(
        body, out_shape=out_shape, grid=(T // tm,), in_specs=in_specs, out_specs=out_specs,
        scratch_shapes=list(scratch), compiler_params=_cparams(1), name=name,
    )(*row_args, *const_args)
    return res


def _mix_proj_body(x_ref, xp_ref, mu_ref, w_ref, o_ref):
    x = x_ref[...]
    xs = x + (xp_ref[...] - x) * mu_ref[...]
    o_ref[...] = _dot(xs.astype(BF16), w_ref[...])


def _lora_body(has_v, x_ref, xp_ref, mu_ref, w0_ref, wa_ref, wb_ref, a0_ref, aa_ref, ab_ref,
               ga_ref, gb_ref, *rest):
    if has_v:
        v0_ref, va_ref, vb_ref, logw_ref, a_ref, g_ref, vg_ref = rest
    else:
        logw_ref, a_ref, g_ref = rest
    x = x_ref[...]
    dx = xp_ref[...] - x
    mix = lambda c: (x + dx * mu_ref[c:c + 1, :]).astype(BF16)
    w_raw = w0_ref[...] + _dot(jnp.tanh(_dot(mix(3), wa_ref[...])).astype(BF16), wb_ref[...])
    logw_ref[...] = (-0.6065306597126334) * jax.nn.sigmoid(w_raw)
    a_ref[...] = jax.nn.sigmoid(a0_ref[...] + _dot(_dot(mix(4), aa_ref[...]).astype(BF16), ab_ref[...]))
    g_ref[...] = _dot(jax.nn.sigmoid(_dot(mix(5), ga_ref[...])).astype(BF16), gb_ref[...])
    if has_v:
        vg_ref[...] = jax.nn.sigmoid(v0_ref[...] + _dot(_dot(mix(2), va_ref[...]).astype(BF16), vb_ref[...]))


def _out_proj_ln_body(alpha, y_ref, g_ref, x_ref, w_ref, lg_ref, lb_ref, o_ref):
    h = _dot((y_ref[...].astype(F32) * g_ref[...].astype(F32)).astype(BF16), w_ref[...])
    o_ref[...] = _layer_norm(alpha * x_ref[...] + h, lg_ref[...], lb_ref[...])


def _proj_rms_body(scale, x_ref, w_ref, gain_ref, o_ref):
    acc = _dot(x_ref[...].astype(BF16), w_ref[...])
    for h in range(acc.shape[1] // FOX_HEAD):
        sl = slice(h * FOX_HEAD, (h + 1) * FOX_HEAD)
        t = acc[:, sl]
        ms = jnp.mean(t * t, axis=-1, keepdims=True)
        o_ref[:, sl] = (t * lax.rsqrt(ms + RMS_EPS) * gain_ref[...] * scale).astype(o_ref.dtype)


def _proj_body(x_ref, w_ref, o_ref):
    o_ref[...] = _dot(x_ref[...].astype(BF16), w_ref[...]).astype(o_ref.dtype)


def _proj_sigmoid_body(x_ref, w_ref, o_ref):
    o_ref[...] = jax.nn.sigmoid(_dot(x_ref[...].astype(BF16), w_ref[...])).astype(o_ref.dtype)


def _log_sigmoid(z):
    return jnp.minimum(z, 0.0) - jnp.log(1.0 + jnp.exp(-jnp.abs(z)))


def _forget_cumsum_body(x_ref, w_ref, b_ref, c_ref, carry_ref):
    @pl.when(pl.program_id(0) == 0)
    def _():
        carry_ref[...] = jnp.zeros_like(carry_ref)

    logf = _log_sigmoid(_dot(x_ref[...].astype(BF16), w_ref[...]) + b_ref[...])
    tm = logf.shape[0]
    tri = (lax.broadcasted_iota(jnp.int32, (tm, tm), 0) >= lax.broadcasted_iota(jnp.int32, (tm, tm), 1))
    tri = tri.astype(BF16)
    hi = logf.astype(BF16)
    r1 = logf - hi.astype(F32)
    mid = r1.astype(BF16)
    lo = (r1 - mid.astype(F32)).astype(BF16)
    c = _dot(tri, hi) + _dot(tri, mid) + _dot(tri, lo) + carry_ref[...]
    c_ref[...] = c
    carry_ref[...] = c[tm - 1:tm, :]


def _half_sum(x, lo):
    s_lo = jnp.sum(jnp.where(lo, x, 0.0), axis=-1, keepdims=True)
    s_hi = jnp.sum(jnp.where(lo, 0.0, x), axis=-1, keepdims=True)
    return jnp.where(lo, s_lo, s_hi)


def _to_dd(x, lo, roll):
    xr = roll(x, RWKV_HEAD, 1)
    return jnp.concatenate([jnp.where(lo, x, xr), jnp.where(lo, xr, x)], axis=0)


def _chunk_phase1(r, logw, k_raw, v, a, k_k, k_a, r_k, roll):
    C = r.shape[0]
    lane = lax.broadcasted_iota(jnp.int32, (C, LANES), 1)
    lo = lane < RWKV_HEAD
    kk = k_raw * k_k
    kk = kk / jnp.maximum(jnp.sqrt(_half_sum(kk * kk, lo)), 1e-12)
    k = k_raw * (1.0 + (a - 1.0) * k_a)
    bonus = _half_sum(r * k * r_k, lo) * v
    b = kk * a
    row = lax.broadcasted_iota(jnp.int32, (C, LANES), 0)
    cw = logw
    s = 1
    while s < C:
        cw = cw + jnp.where(row >= s, roll(cw, s, 0), 0.0)
        s *= 2
    cw_end = cw[C - 1:C, :]
    g_in = jnp.exp(cw)
    g_ex = jnp.exp(cw - logw)
    g_inv = jnp.exp(-cw)
    g_rem = jnp.exp(cw_end - cw)
    dd = lambda x: _to_dd(x, lo, roll)
    a_t = dd(-kk * g_ex)
    r_t = dd(r * g_in)
    k_h = dd(k * g_inv)
    b_h = dd(b * g_inv)
    k_g = dd(k * g_rem)
    b_g = dd(b * g_rem)
    v_d = dd(v)
    ri = lax.broadcasted_iota(jnp.int32, (2 * C, 2 * C), 0)
    ci = lax.broadcasted_iota(jnp.int32, (2 * C, 2 * C), 1)
    same = (ri < C) == (ci < C)
    strict = same & (ri > ci)
    incl = same & (ri >= ci)
    ar = jnp.concatenate([a_t, r_t], axis=0).astype(BF16)
    kb = jnp.concatenate([k_h, b_h], axis=0).astype(BF16)
    prod = 0.5 * _dot_nt(ar, kb)
    a_ak = jnp.where(strict, prod[:2 * C, :2 * C], 0.0)
    a_ab = jnp.where(strict, prod[:2 * C, 2 * C:], 0.0)
    a_rk = jnp.where(incl, prod[2 * C:, :2 * C], 0.0)
    a_rb = jnp.where(incl, prod[2 * C:, 2 * C:], 0.0)
    eye = (ri == ci).astype(F32)
    tinv = eye + a_ab
    npow = a_ab
    s = 2
    while s < C:
        npow = _dot3(npow, npow)
        tinv = tinv + _dot3(tinv, npow)
        s *= 2
    w1 = _dot(a_ak.astype(BF16), v_d.astype(BF16))
    pq = _dot3(tinv, jnp.concatenate([a_t, w1], axis=1))
    p = pq[:, :LANES]
    q = pq[:, LANES:]
    a_rb16 = a_rb.astype(BF16)
    g = r_t + _dot(a_rb16, p.astype(BF16))
    yi = _dot(a_rk.astype(BF16), v_d.astype(BF16)) + _dot(a_rb16, q.astype(BF16))
    kbg_t = jnp.concatenate([jnp.where(same, k_g, 0.0), jnp.where(same, b_g, 0.0)], axis=0).T
    z = _dot(kbg_t.astype(BF16), jnp.concatenate([v_d, q], axis=0).astype(BF16))
    m = _dot(kbg_t[:, 2 * C:].astype(BF16), p.astype(BF16))
    m_bd = jnp.where(same, m, 0.0) + jnp.where(ri == ci, jnp.exp(cw_end), 0.0)
    g_bd = jnp.where(same, g, 0.0)
    yi_lp = jnp.where(lo, yi[:C], yi[C:])
    return g_bd, m_bd, z, yi_lp, bonus


def _rwkv_phase1_body(npairs, has_v, r_ref, k_ref, v_ref, lw_ref, a_ref, *rest):
    if has_v:
        vf_ref, vg_ref, kk_ref, ka_ref, rk_ref, g_ref, m_ref, z_ref, yi_ref, bo_ref, vout_ref = rest
    else:
        kk_ref, ka_ref, rk_ref, g_ref, m_ref, z_ref, yi_ref, bo_ref = rest
    roll = lambda x, s, ax: pltpu.roll(x, s, ax)
    for p in range(npairs):
        sl = slice(p * LANES, (p + 1) * LANES)
        v = v_ref[:, sl]
        if has_v:
            v = v + (vf_ref[:, sl] - v) * vg_ref[:, sl]
            vout_ref[:, sl] = v
        g_bd, m_bd, z, yi, bonus = _chunk_phase1(
            r_ref[:, sl], lw_ref[:, sl], k_ref[:, sl], v, a_ref[:, sl],
            kk_ref[:, sl], ka_ref[:, sl], rk_ref[:, sl], roll)
        g_ref[0, :, sl] = g_bd
        m_ref[0, :, sl] = m_bd
        z_ref[0, :, sl] = z
        yi_ref[:, sl] = yi
        bo_ref[:, sl] = bonus


def _rwkv_phase1(r, k, v, logw, a, v_first, vgate, k_k, k_a, r_k, *, pairs_per_step=4):
    T, D = r.shape
    C = CHUNK
    n_chunks = T // C
    n_pairs = D // LANES
    pp = min(pairs_per_step, n_pairs)
    assert n_pairs % pp == 0 and T % C == 0
    has_v = v_first is not None
    W = pp * LANES
    row_spec = pl.BlockSpec((C, W), lambda c, j: (c, j))
    par_spec = pl.BlockSpec((1, W), lambda c, j: (0, j))
    mat_spec = pl.BlockSpec((1, 2 * C, W), lambda c, j: (c, 0, j))
    ins = [r, k, v, logw, a] + ([v_first, vgate] if has_v else []) + [k_k, k_a, r_k]
    in_specs = [row_spec] * (7 if has_v else 5) + [par_spec] * 3
    mat_shape = jax.ShapeDtypeStruct((n_chunks, 2 * C, D), F32)
    row_shape = jax.ShapeDtypeStruct((T, D), F32)
    out_shape = [mat_shape, mat_shape, mat_shape, row_shape, row_shape] + ([row_shape] if has_v else [])
    out_specs = [mat_spec, mat_spec, mat_spec, row_spec, row_spec] + ([row_spec] if has_v else [])
    return pl.pallas_call(
        functools.partial(_rwkv_phase1_body, pp, has_v),
        out_shape=out_shape, grid=(n_chunks, n_pairs // pp), in_specs=in_specs, out_specs=out_specs,
        compiler_params=_cparams(2), name="rwkv_phase1",
    )(*ins)


def _rwkv_phase2_body(npairs, g_ref, m_ref, z_ref, yi_ref, bo_ref, lg_ref, lb_ref, y_ref, s_ref):
    @pl.when(pl.program_id(1) == 0)
    def _():
        s_ref[...] = jnp.zeros_like(s_ref)

    C = CHUNK
    lane = lax.broadcasted_iota(jnp.int32, (C, LANES), 1)
    lo = lane < RWKV_HEAD
    for p in range(npairs):
        sl = slice(p * LANES, (p + 1) * LANES)
        s0 = s_ref[p]
        gm = jnp.concatenate([g_ref[0, :, sl], m_ref[0, :, sl]], axis=0)
        res = _dot3(gm, s0)
        s_ref[p] = res[2 * C:] + z_ref[0, :, sl]
        yd = res[:2 * C]
        y = jnp.where(lo, yd[:C], yd[C:]) + yi_ref[:, sl]
        mean = _half_sum(y, lo) * (1.0 / RWKV_HEAD)
        yc = y - mean
        var = _half_sum(yc * yc, lo) * (1.0 / RWKV_HEAD)
        y_ref[:, sl] = yc * lax.rsqrt(var + GN_EPS) * lg_ref[:, sl] + lb_ref[:, sl] + bo_ref[:, sl]


def _rwkv_phase2(g_bd, m_bd, z, yi, bonus, lnx_g, lnx_b, *, pairs_per_step=4):
    n_chunks, C2, D = g_bd.shape
    C = CHUNK
    T = n_chunks * C
    n_pairs = D // LANES
    pp = min(pairs_per_step, n_pairs)
    W = pp * LANES
    row_spec = pl.BlockSpec((C, W), lambda j, c: (c, j))
    par_spec = pl.BlockSpec((1, W), lambda j, c: (0, j))
    mat_spec = pl.BlockSpec((1, 2 * C, W), lambda j, c: (c, 0, j))
    return pl.pallas_call(
        functools.partial(_rwkv_phase2_body, pp),
        out_shape=jax.ShapeDtypeStruct((T, D), F32),
        grid=(n_pairs // pp, n_chunks),
        in_specs=[mat_spec, mat_spec, mat_spec, row_spec, row_spec, par_spec, par_spec],
        out_specs=row_spec,
        scratch_shapes=[pltpu.VMEM((pp, 2 * C, LANES), F32)],
        compiler_params=_cparams(2), name="rwkv_phase2",
    )(g_bd, m_bd, z, yi, bonus, lnx_g, lnx_b)


def _rwkv_layer(x, v_first, vres, mu, w_rkv, w0, w_a, w_b, a0, a_a, a_b, g_a, g_b, k_k, k_a, r_k,
                lnx_g, lnx_b, w_o, ln_g, ln_b, alpha):
    T, D = x.shape
    xp = jnp.concatenate([jnp.zeros((1, D), x.dtype), x[:-1]], axis=0)
    row = lambda t: t.reshape(1, -1)
    padc = lambda w: jnp.pad(w, ((0, 0), (0, (-w.shape[1]) % LANES))).astype(BF16)
    padr = lambda w: jnp.pad(w, ((0, (-w.shape[0]) % LANES), (0, 0))).astype(BF16)
    rkv = [
        _rowwise(_mix_proj_body, [x, xp], [row(mu[c]), w_rkv[c].astype(BF16)], [(D, F32)],
                 tm=256, name=f"rwkv_proj{c}")[0]
        for c in range(3)
    ]
    consts = [mu, row(w0), padc(w_a), padr(w_b), row(a0), padc(a_a), padr(a_b), padc(g_a), padr(g_b)]
    outs = [(D, F32)] * 3
    has_v = vres is not None
    if has_v:
        v0, v_a, v_b = vres
        consts += [row(v0), padc(v_a), padr(v_b)]
        outs = outs + [(D, F32)]
    lora = _rowwise(functools.partial(_lora_body, has_v), [x, xp], consts, outs, tm=256, name="rwkv_lora")
    logw, a, g = lora[:3]
    vgate = lora[3] if has_v else None
    ph1 = _rwkv_phase1(rkv[0], rkv[1], rkv[2], logw, a, v_first if has_v else None, vgate,
                       row(k_k), row(k_a), r_k.reshape(1, D))
    g_bd, m_bd, z, yi, bonus = ph1[:5]
    if not has_v:
        v_first = rkv[2]
    y = _rwkv_phase2(g_bd, m_bd, z, yi, bonus, row(lnx_g), row(lnx_b))
    x_new = _rowwise(functools.partial(_out_proj_ln_body, alpha), [y, g, x],
                     [w_o.astype(BF16), row(ln_g), row(ln_b)], [(D, F32)], tm=256, name="rwkv_out")[0]
    return x_new, v_first


def _fox_flash_body(tq, tk, q_ref, k_ref, v_ref, cq_ref, ck_ref, o_ref, m_ref, l_ref, acc_ref):
    h = pl.program_id(0)
    i = pl.program_id(1)
    j = pl.program_id(2)
    nk = pl.num_programs(2)

    @pl.when(j == 0)
    def _():
        m_ref[...] = jnp.full_like(m_ref, -1e30)
        l_ref[...] = jnp.zeros_like(l_ref)
        acc_ref[...] = jnp.zeros_like(acc_ref)

    @pl.when(j * tk <= i * tq + (tq - 1))
    def _():
        s = _dot_nt(q_ref[...], k_ref[...])
        lane = lax.broadcasted_iota(jnp.int32, cq_ref.shape, 1)
        cq = jnp.sum(jnp.where(lane == h, cq_ref[...], 0.0), axis=-1, keepdims=True)
        s = s + (cq - ck_ref[0])
        qpos = i * tq + lax.broadcasted_iota(jnp.int32, (tq, tk), 0)
        kpos = j * tk + lax.broadcasted_iota(jnp.int32, (tq, tk), 1)
        s = jnp.where(kpos <= qpos, s, -1e30)
        m_old = m_ref[...]
        m_new = jnp.maximum(m_old, jnp.max(s, axis=-1, keepdims=True))
        alpha = jnp.exp(m_old - m_new)
        p = jnp.exp(s - m_new)
        l_ref[...] = alpha * l_ref[...] + jnp.sum(p, axis=-1, keepdims=True)
        acc_ref[...] = alpha * acc_ref[...] + _dot(p.astype(BF16), v_ref[...])
        m_ref[...] = m_new

    @pl.when(j == nk - 1)
    def _():
        o_ref[...] = (acc_ref[...] / l_ref[...]).astype(o_ref.dtype)


def _fox_flash(q, k, v, c_col, c_row, *, tq=512, tk=512):
    T, D = q.shape
    H = D // FOX_HEAD
    tq = min(tq, T)
    tk = min(tk, T)
    last_k = lambda i: (i * tq + tq - 1) // tk
    kv_spec = pl.BlockSpec((tk, FOX_HEAD), lambda h, i, j: (jnp.minimum(j, last_k(i)), h))
    return pl.pallas_call(
        functools.partial(_fox_flash_body, tq, tk),
        out_shape=jax.ShapeDtypeStruct((T, D), F32),
        grid=(H, T // tq, T // tk),
        in_specs=[
            pl.BlockSpec((tq, FOX_HEAD), lambda h, i, j: (i, h)),
            kv_spec, kv_spec,
            pl.BlockSpec((tq, LANES), lambda h, i, j: (i, 0)),
            pl.BlockSpec((1, 1, tk), lambda h, i, j: (h, 0, jnp.minimum(j, last_k(i)))),
        ],
        out_specs=pl.BlockSpec((tq, FOX_HEAD), lambda h, i, j: (i, h)),
        scratch_shapes=[pltpu.VMEM((tq, 1), F32), pltpu.VMEM((tq, 1), F32), pltpu.VMEM((tq, FOX_HEAD), F32)],
        compiler_params=_cparams(3), name="fox_flash",
    )(q, k, v, c_col, c_row)


def _fox_shared_kv(x, w_kvf, b_f, k_norm):
    T, D = x.shape
    H = D // FOX_HEAD
    gain = k_norm.reshape(1, FOX_HEAD)
    k = _rowwise(functools.partial(_proj_rms_body, 1.0), [x], [w_kvf[:, :D].astype(BF16), gain],
                 [(D, BF16)], tm=256, name="fox_k")[0]
    v = _rowwise(_proj_body, [x], [w_kvf[:, D:2 * D].astype(BF16)], [(D, BF16)], tm=256, name="fox_v")[0]
    w_f = jnp.pad(w_kvf[:, 2 * D:], ((0, 0), (0, LANES - H))).astype(BF16)
    b_pad = jnp.pad(b_f, (0, LANES - H)).reshape(1, LANES)
    c = _rowwise(_forget_cumsum_body, [x], [w_f, b_pad], [(LANES, F32)], tm=256, name="fox_forget",
                 scratch=[pltpu.VMEM((1, LANES), F32)])[0]
    c_row = c[:, :H].T.reshape(H, 1, T)
    return k, v, c, c_row


def _fox_layer(x, k, v, c_col, c_row, w_qg, q_norm, w_o, ln_g, ln_b, alpha):
    T, D = x.shape
    row = lambda t: t.reshape(1, -1)
    q = _rowwise(functools.partial(_proj_rms_body, FOX_HEAD ** -0.5), [x],
                 [w_qg[:, :D].astype(BF16), q_norm.reshape(1, FOX_HEAD)], [(D, BF16)], tm=256, name="fox_q")[0]
    gate = _rowwise(_proj_sigmoid_body, [x], [w_qg[:, D:].astype(BF16)], [(D, F32)], tm=256, name="fox_gate")[0]
    o = _fox_flash(q, k, v, c_col, c_row)
    return _rowwise(functools.partial(_out_proj_ln_body, alpha), [o, gate, x],
                    [w_o.astype(BF16), row(ln_g), row(ln_b)], [(D, F32)], tm=256, name="fox_out")[0]


def _router_body(x_ref, w_ref, b_ref, o_ref):
    logits = _dot3(x_ref[...], w_ref[...])
    scores = jax.nn.sigmoid(logits)
    biased = scores + b_ref[...]
    sc = [scores[:, e:e + 1] for e in range(N_EXPERTS)]
    bi = [biased[:, e:e + 1] for e in range(N_EXPERTS)]
    gs = []
    for g in range(N_GROUPS):
        a, b, c, d = bi[4 * g:4 * g + 4]
        hi1, lo1 = jnp.maximum(a, b), jnp.minimum(a, b)
        hi2, lo2 = jnp.maximum(c, d), jnp.minimum(c, d)
        gs.append(jnp.maximum(hi1, hi2) + jnp.maximum(jnp.minimum(hi1, hi2), jnp.maximum(lo1, lo2)))
    best, g_sel = gs[0], jnp.zeros_like(gs[0], dtype=jnp.int32)
    for g in range(1, N_GROUPS):
        better = gs[g] > best
        best = jnp.where(better, gs[g], best)
        g_sel = jnp.where(better, g, g_sel)

    def pick(cols, j):
        out = cols[j]
        for g in range(1, N_GROUPS):
            out = jnp.where(g_sel == g, cols[4 * g + j], out)
        return out

    ib = [pick(bi, j) for j in range(EXPERTS_PER_GROUP)]
    isc = [pick(sc, j) for j in range(EXPERTS_PER_GROUP)]

    def argmax4(vals):
        bv, bi_, bs = vals[0], jnp.zeros_like(g_sel), isc[0]
        for j in range(1, EXPERTS_PER_GROUP):
            better = vals[j] > bv
            bv = jnp.where(better, vals[j], bv)
            bi_ = jnp.where(better, j, bi_)
            bs = jnp.where(better, isc[j], bs)
        return bi_, bs

    i1, s1 = argmax4(ib)
    ib2 = [jnp.where(i1 == j, -jnp.inf, ib[j]) for j in range(EXPERTS_PER_GROUP)]
    i2, s2 = argmax4(ib2)
    den = s1 + s2
    e1 = (g_sel * EXPERTS_PER_GROUP + i1).astype(F32)
    e2 = (g_sel * EXPERTS_PER_GROUP + i2).astype(F32)
    lane = lax.broadcasted_iota(jnp.int32, o_ref.shape, 1)
    o_ref[...] = jnp.where(lane == 0, e1, jnp.where(lane == 1, e2, jnp.where(lane == 2, s1 / den, s2 / den)))


def _moe_expert_body(tm, tile_e_ref, src_ref, nt_ref, x_hbm, win_ref, wout_ref, rw_ref, o_ref, xbuf, sem):
    i = pl.program_id(0)

    @pl.when(i < nt_ref[0])
    def _():
        def row_copy(r):
            tok = src_ref[i * tm + r]
            return pltpu.make_async_copy(x_hbm.at[pl.ds(tok, 1), :], xbuf.at[pl.ds(r, 1), :], sem.at[0])

        def issue(r, carry):
            row_copy(r).start()
            return carry

        lax.fori_loop(0, tm, issue, 0)

        def drain(r, carry):
            row_copy(r).wait()
            return carry

        lax.fori_loop(0, tm, drain, 0)
        F = wout_ref.shape[1]
        h = _dot(xbuf[...].astype(BF16), win_ref[0])
        act = h[:, :F] * jax.nn.silu(h[:, F:])
        o_ref[...] = _dot(act.astype(BF16), wout_ref[0]) * rw_ref[...]

    @pl.when(i >= nt_ref[0])
    def _():
        o_ref[...] = jnp.zeros_like(o_ref)


def _moe_combine_body(tm, alpha, pos_ref, y_hbm, x_ref, lg_ref, lb_ref, o_ref, buf, sem):
    i = pl.program_id(0)

    def row_copy(r, j):
        p = pos_ref[2 * (i * tm + r) + j]
        return pltpu.make_async_copy(y_hbm.at[pl.ds(p, 1), :], buf.at[j, pl.ds(r, 1), :], sem.at[j])

    def issue(r, carry):
        row_copy(r, 0).start()
        row_copy(r, 1).start()
        return carry

    lax.fori_loop(0, tm, issue, 0)

    def drain(r, carry):
        row_copy(r, 0).wait()
        row_copy(r, 1).wait()
        return carry

    lax.fori_loop(0, tm, drain, 0)
    z = alpha * x_ref[...] + (buf[0] + buf[1])
    o_ref[...] = _layer_norm(z, lg_ref[...], lb_ref[...])


def _moe_layer(x, router_w, router_b, w_in, w_out, ln_g, ln_b, alpha, *, tm=256):
    T, D = x.shape
    E, F = w_out.shape[0], w_out.shape[1]
    tm = min(tm, T)
    rw_pad = jnp.pad(router_w, ((0, 0), (0, LANES - E)))
    rb_pad = jnp.pad(router_b, (0, LANES - E)).reshape(1, LANES)
    route = _rowwise(_router_body, [x], [rw_pad, rb_pad], [(LANES, F32)], tm=256, name="moe_router")[0]
    e_flat = route[:, :2].astype(jnp.int32).reshape(-1)
    w_flat = route[:, 2:4].reshape(-1)
    onehot = (e_flat[:, None] == jnp.arange(E)[None, :]).astype(jnp.int32)
    rank = jnp.take_along_axis(jnp.cumsum(onehot, axis=0) - onehot, e_flat[:, None], axis=1)[:, 0]
    counts = jnp.sum(onehot, axis=0)
    padded = ((counts + tm - 1) // tm) * tm
    ends = jnp.cumsum(padded)
    starts = ends - padded
    dest = starts[e_flat] + rank
    n_rows = 2 * T + E * tm
    n_tiles = n_rows // tm
    src_tok = jnp.zeros((n_rows,), jnp.int32).at[dest].set(jnp.arange(2 * T, dtype=jnp.int32) // 2)
    row_w = jnp.zeros((n_rows, 1), F32).at[dest, 0].set(w_flat)
    used_tiles = (ends[-1] // tm).astype(jnp.int32)
    tile_start = jnp.minimum(jnp.arange(n_tiles, dtype=jnp.int32), used_tiles - 1) * tm
    tile_e = jnp.minimum(jnp.searchsorted(ends, tile_start, side="right"), E - 1).astype(jnp.int32)

    y_sorted = pl.pallas_call(
        functools.partial(_moe_expert_body, tm),
        out_shape=jax.ShapeDtypeStruct((n_rows, D), F32),
        grid_spec=pltpu.PrefetchScalarGridSpec(
            num_scalar_prefetch=3, grid=(n_tiles,),
            in_specs=[
                pl.BlockSpec(memory_space=pl.ANY),
                pl.BlockSpec((1, D, 2 * F), lambda i, te, src, nt: (te[i], 0, 0)),
                pl.BlockSpec((1, F, D), lambda i, te, src, nt: (te[i], 0, 0)),
                pl.BlockSpec((tm, 1), lambda i, te, src, nt: (i, 0)),
            ],
            out_specs=pl.BlockSpec((tm, D), lambda i, te, src, nt: (i, 0)),
            scratch_shapes=[pltpu.VMEM((tm, D), F32), pltpu.SemaphoreType.DMA((1,))],
        ),
        compiler_params=_cparams(1), name="moe_experts",
    )(tile_e, src_tok, used_tiles.reshape(1), x, w_in.astype(BF16), w_out.astype(BF16), row_w)

    row = lambda t: t.reshape(1, -1)
    return pl.pallas_call(
        functools.partial(_moe_combine_body, tm, alpha),
        out_shape=jax.ShapeDtypeStruct((T, D), F32),
        grid_spec=pltpu.PrefetchScalarGridSpec(
            num_scalar_prefetch=1, grid=(T // tm,),
            in_specs=[
                pl.BlockSpec(memory_space=pl.ANY),
                pl.BlockSpec((tm, D), lambda i, pos: (i, 0)),
                pl.BlockSpec((1, D), lambda i, pos: (0, 0)),
                pl.BlockSpec((1, D), lambda i, pos: (0, 0)),
            ],
            out_specs=pl.BlockSpec((tm, D), lambda i, pos: (i, 0)),
            scratch_shapes=[pltpu.VMEM((2, tm, D), F32), pltpu.SemaphoreType.DMA((2,))],
        ),
        compiler_params=_cparams(1), name="moe_combine",
    )(dest.astype(jnp.int32), y_sorted, x, row(ln_g), row(ln_b))


def kernel(x, rw_mu, rw_w_rkv, rw_w0, rw_w_a, rw_w_b, rw_a0, rw_a_a, rw_a_b, rw_v0, rw_v_a, rw_v_b, rw_g_a, rw_g_b, rw_k_k, rw_k_a, rw_r_k, rw_lnx_g, rw_lnx_b, rw_w_o, fx_w_kvf, fx_b_f, fx_k_norm, fx_w_qg, fx_q_norm, fx_w_o, router_w, router_b, moe_w_in, moe_w_out, ln_g, ln_b):
    B, T, D = x.shape
    depth = ln_g.shape[0]
    n_a = rw_mu.shape[0]
    alpha = (2 * depth) ** 0.25
    outs = []
    for bi in range(B):
        h = x[bi]
        v_first = None
        kv = None
        for l in range(depth):
            if l < n_a:
                vres = None if l == 0 else (rw_v0[l - 1], rw_v_a[l - 1], rw_v_b[l - 1])
                h, v_first = _rwkv_layer(
                    h, v_first, vres, rw_mu[l], rw_w_rkv[l], rw_w0[l], rw_w_a[l], rw_w_b[l], rw_a0[l],
                    rw_a_a[l], rw_a_b[l], rw_g_a[l], rw_g_b[l], rw_k_k[l], rw_k_a[l], rw_r_k[l],
                    rw_lnx_g[l], rw_lnx_b[l], rw_w_o[l], ln_g[l, 0], ln_b[l, 0], alpha)
            else:
                if kv is None:
                    kv = _fox_shared_kv(h, fx_w_kvf, fx_b_f, fx_k_norm)
                j = l - n_a
                h = _fox_layer(h, *kv, fx_w_qg[j], fx_q_norm[j], fx_w_o[j], ln_g[l, 0], ln_b[l, 0], alpha)
            h = _moe_layer(h, router_w, router_b, moe_w_in[l], moe_w_out[l], ln_g[l, 1], ln_b[l, 1], alpha)
        outs.append(h)
    return jnp.stack(outs, axis=0)
```

```python
import functools

import jax
import jax.numpy as jnp
from jax import lax
from jax.experimental import pallas as pl
from jax.experimental.pallas import tpu as pltpu

F32 = jnp.float32
BF16 = jnp.bfloat16

RWKV_HEAD = 64
FOX_HEAD = 128
N_EXPERTS = 16
N_GROUPS = 4
EXPERTS_PER_GROUP = N_EXPERTS // N_GROUPS
GN_EPS = 64e-5
LN_EPS = 1e-5
RMS_EPS = 1e-6
CHUNK = 64
LANES = 128
LOG2E = 1.4426950408889634
VMEM_LIMIT = 56 * 1024 * 1024


def _cparams(n_axes, vmem=VMEM_LIMIT):
    return pltpu.CompilerParams(dimension_semantics=("arbitrary",) * n_axes, vmem_limit_bytes=vmem)


def _dot(a, b):
    return jnp.dot(a, b, preferred_element_type=F32)


def _dot_nt(a, b):
    return lax.dot_general(a, b, (((1,), (1,)), ((), ())), preferred_element_type=F32)


def _split(a):
    hi = a.astype(BF16)
    lo = (a - hi.astype(F32)).astype(BF16)
    return hi, lo


def _dot3(a, b):
    ah, al = _split(a)
    bh, bl = _split(b)
    return _dot(ah, bh) + _dot(ah, bl) + _dot(al, bh)


def _layer_norm(z, g, b):
    mu = jnp.mean(z, axis=-1, keepdims=True)
    zc = z - mu
    var = jnp.mean(zc * zc, axis=-1, keepdims=True)
    return zc * lax.rsqrt(var + LN_EPS) * g + b


def _rowwise(body, row_args, const_args, outs, *, tm, name, scratch=()):
    T = row_args[0].shape[0]
    tm = min(tm, T)
    assert T % tm == 0
    in_specs = [pl.BlockSpec((tm, a.shape[1]), lambda i: (i, 0)) for a in row_args]
    in_specs += [pl.BlockSpec(a.shape, lambda i, nd=a.ndim: (0,) * nd) for a in const_args]
    out_shape = [jax.ShapeDtypeStruct((T, w), dt) for (w, dt) in outs]
    out_specs = [pl.BlockSpec((tm, w), lambda i: (i, 0)) for (w, dt) in outs]
    res = pl.pallas_call(
        body, out_shape=out_shape, grid=(T // tm,), in_specs=in_specs, out_specs=out_specs,
        scratch_shapes=list(scratch), compiler_params=_cparams(1), name=name,
    )(*row_args, *const_args)
    return res


def _mix_proj_body(x_ref, xp_ref, mu_ref, w_ref, o_ref):
    x = x_ref[...]
    xs = x + (xp_ref[...] - x) * mu_ref[...]
    o_ref[...] = _dot(xs.astype(BF16), w_ref[...])


def _lora_body(has_v, x_ref, xp_ref, mu_ref, w0_ref, wa_ref, wb_ref, a0_ref, aa_ref, ab_ref,
               ga_ref, gb_ref, *rest):
    if has_v:
        v0_ref, va_ref, vb_ref, logw_ref, a_ref, g_ref, vg_ref = rest
    else:
        logw_ref, a_ref, g_ref = rest
    x = x_ref[...]
    dx = xp_ref[...] - x
    mix = lambda c: (x + dx * mu_ref[c:c + 1, :]).astype(BF16)
    w_raw = w0_ref[...] + _dot(jnp.tanh(_dot(mix(3), wa_ref[...])).astype(BF16), wb_ref[...])
    logw_ref[...] = (-0.6065306597126334) * jax.nn.sigmoid(w_raw)
    a_ref[...] = jax.nn.sigmoid(a0_ref[...] + _dot(_dot(mix(4), aa_ref[...]).astype(BF16), ab_ref[...]))
    g_ref[...] = _dot(jax.nn.sigmoid(_dot(mix(5), ga_ref[...])).astype(BF16), gb_ref[...])
    if has_v:
        vg_ref[...] = jax.nn.sigmoid(v0_ref[...] + _dot(_dot(mix(2), va_ref[...]).astype(BF16), vb_ref[...]))


def _out_proj_ln_body(alpha, y_ref, g_ref, x_ref, w_ref, lg_ref, lb_ref, o_ref):
    h = _dot((y_ref[...].astype(F32) * g_ref[...].astype(F32)).astype(BF16), w_ref[...])
    o_ref[...] = _layer_norm(alpha * x_ref[...] + h, lg_ref[...], lb_ref[...])


def _proj_rms_aug_body(scale, is_query, x_ref, c_ref, w_ref, gain_ref, o_ref):
    acc = _dot(x_ref[...].astype(BF16), w_ref[...])
    c2 = c_ref[...] * LOG2E
    lane = lax.broadcasted_iota(jnp.int32, c2.shape, 1)
    for h in range(acc.shape[1] // FOX_HEAD):
        t = acc[:, h * FOX_HEAD:(h + 1) * FOX_HEAD]
        ms = jnp.mean(t * t, axis=-1, keepdims=True)
        base = 2 * h * FOX_HEAD
        o_ref[:, base:base + FOX_HEAD] = (t * lax.rsqrt(ms + RMS_EPS) * gain_ref[...] * scale).astype(o_ref.dtype)
        ch = jnp.sum(jnp.where(lane == h, c2, 0.0), axis=-1, keepdims=True)
        if not is_query:
            ch = -ch
        hi = ch.astype(BF16).astype(F32)
        mid = (ch - hi).astype(BF16).astype(F32)
        low = ch - hi - mid
        t0 = 0 if is_query else 3
        one = ((lane >= 3 - t0) & (lane < 6 - t0)).astype(F32)
        aug = jnp.where(lane == t0, hi, jnp.where(lane == t0 + 1, mid, jnp.where(lane == t0 + 2, low, one)))
        o_ref[:, base + FOX_HEAD:base + 2 * FOX_HEAD] = aug.astype(o_ref.dtype)


def _proj_body(x_ref, w_ref, o_ref):
    o_ref[...] = _dot(x_ref[...].astype(BF16), w_ref[...]).astype(o_ref.dtype)


def _proj_sigmoid_body(x_ref, w_ref, o_ref):
    o_ref[...] = jax.nn.sigmoid(_dot(x_ref[...].astype(BF16), w_ref[...])).astype(o_ref.dtype)


def _log_sigmoid(z):
    return jnp.minimum(z, 0.0) - jnp.log(1.0 + jnp.exp(-jnp.abs(z)))


def _forget_cumsum_body(x_ref, w_ref, b_ref, c_ref, carry_ref):
    @pl.when(pl.program_id(0) == 0)
    def _():
        carry_ref[...] = jnp.zeros_like(carry_ref)

    logf = _log_sigmoid(_dot(x_ref[...].astype(BF16), w_ref[...]) + b_ref[...])
    tm = logf.shape[0]
    tri = (lax.broadcasted_iota(jnp.int32, (tm, tm), 0) >= lax.broadcasted_iota(jnp.int32, (tm, tm), 1))
    tri = tri.astype(BF16)
    hi = logf.astype(BF16)
    r1 = logf - hi.astype(F32)
    mid = r1.astype(BF16)
    lo = (r1 - mid.astype(F32)).astype(BF16)
    c = _dot(tri, hi) + _dot(tri, mid) + _dot(tri, lo) + carry_ref[...]
    c_ref[...] = c
    carry_ref[...] = c[tm - 1:tm, :]


def _half_sum(x, lo):
    s_lo = jnp.sum(jnp.where(lo, x, 0.0), axis=-1, keepdims=True)
    s_hi = jnp.sum(jnp.where(lo, 0.0, x), axis=-1, keepdims=True)
    return jnp.where(lo, s_lo, s_hi)


def _to_dd(x, lo, roll):
    xr = roll(x, RWKV_HEAD, 1)
    return jnp.concatenate([jnp.where(lo, x, xr), jnp.where(lo, xr, x)], axis=0)


def _interleave(gens):
    gens = list(gens)
    while gens:
        alive = []
        for gen in gens:
            try:
                next(gen)
                alive.append(gen)
            except StopIteration:
                pass
        gens = alive


def _chunk_phase1(r, logw, k_raw, v, a, k_k, k_a, r_k, roll, emit):
    C = r.shape[0]
    lane = lax.broadcasted_iota(jnp.int32, (C, LANES), 1)
    lo = lane < RWKV_HEAD
    kk = k_raw * k_k
    kk = kk / jnp.maximum(jnp.sqrt(_half_sum(kk * kk, lo)), 1e-12)
    k = k_raw * (1.0 + (a - 1.0) * k_a)
    bonus = _half_sum(r * k * r_k, lo) * v
    b = kk * a
    row = lax.broadcasted_iota(jnp.int32, (C, LANES), 0)
    cw = logw
    s = 1
    while s < C:
        cw = cw + jnp.where(row >= s, roll(cw, s, 0), 0.0)
        s *= 2
    cw_end = cw[C - 1:C, :]
    g_in = jnp.exp(cw)
    g_ex = jnp.exp(cw - logw)
    g_inv = jnp.exp(-cw)
    g_rem = jnp.exp(cw_end - cw)
    dd = lambda x: _to_dd(x, lo, roll)
    a_t = dd(-kk * g_ex)
    r_t = dd(r * g_in)
    k_h = dd(k * g_inv)
    b_h = dd(b * g_inv)
    k_g = dd(k * g_rem)
    b_g = dd(b * g_rem)
    v_d = dd(v)
    ri = lax.broadcasted_iota(jnp.int32, (2 * C, 2 * C), 0)
    ci = lax.broadcasted_iota(jnp.int32, (2 * C, 2 * C), 1)
    same = (ri < C) == (ci < C)
    strict = same & (ri > ci)
    incl = same & (ri >= ci)
    ar = jnp.concatenate([a_t, r_t], axis=0).astype(BF16)
    kb = jnp.concatenate([k_h, b_h], axis=0).astype(BF16)
    yield
    prod = 0.5 * _dot_nt(ar, kb)
    yield
    a_ak = jnp.where(strict, prod[:2 * C, :2 * C], 0.0)
    a_ab = jnp.where(strict, prod[:2 * C, 2 * C:], 0.0)
    a_rk = jnp.where(incl, prod[2 * C:, :2 * C], 0.0)
    a_rb = jnp.where(incl, prod[2 * C:, 2 * C:], 0.0)
    v16 = v_d.astype(BF16)
    n16 = a_ab.astype(BF16)
    npow = _dot(n16, n16)
    akv = _dot(jnp.concatenate([a_ak, a_rk], axis=0).astype(BF16), v16)
    yield
    tinv = (ri == ci).astype(F32) + a_ab
    s = 4
    while s < C:
        both = _dot(npow.astype(BF16), jnp.concatenate([tinv, npow], axis=1).astype(BF16))
        yield
        tinv = tinv + both[:, :2 * C]
        npow = both[:, 2 * C:]
        s *= 2
    tinv = tinv + _dot(npow.astype(BF16), tinv.astype(BF16))
    yield
    pq = _dot(tinv.astype(BF16), jnp.concatenate([a_t, akv[:2 * C]], axis=1).astype(BF16))
    yield
    pq16 = pq.astype(BF16)
    rbpq = _dot(a_rb.astype(BF16), pq16)
    kbg_t = jnp.concatenate([jnp.where(same, k_g, 0.0), jnp.where(same, b_g, 0.0)], axis=0).T
    rhs = jnp.concatenate([jnp.concatenate([v16, jnp.zeros_like(v16)], axis=1),
                           jnp.concatenate([pq16[:, LANES:], pq16[:, :LANES]], axis=1)], axis=0)
    zm = _dot(kbg_t.astype(BF16), rhs)
    yield
    g = r_t + rbpq[:, :LANES]
    yi = akv[2 * C:] + rbpq[:, LANES:]
    m_bd = jnp.where(same, zm[:, LANES:], 0.0) + jnp.where(ri == ci, jnp.exp(cw_end), 0.0)
    emit(jnp.where(same, g, 0.0), m_bd, zm[:, :LANES], jnp.where(lo, yi[:C], yi[C:]), bonus)


def _rwkv_phase1_body(npairs, has_v, r_ref, k_ref, v_ref, lw_ref, a_ref, *rest):
    if has_v:
        vf_ref, vg_ref, kk_ref, ka_ref, rk_ref, g_ref, m_ref, z_ref, yi_ref, bo_ref = rest
    else:
        kk_ref, ka_ref, rk_ref, g_ref, m_ref, z_ref, yi_ref, bo_ref = rest
    roll = lambda x, s, ax: pltpu.roll(x, s, ax)

    def one_pair(p):
        sl = slice(p * LANES, (p + 1) * LANES)
        v = v_ref[:, sl]
        if has_v:
            v = v + (vf_ref[:, sl] - v) * vg_ref[:, sl]

        def emit(g_bd, m_bd, z, yi, bonus):
            g_ref[0, :, sl] = g_bd
            m_ref[0, :, sl] = m_bd
            z_ref[0, :, sl] = z
            yi_ref[:, sl] = yi
            bo_ref[:, sl] = bonus

        return _chunk_phase1(r_ref[:, sl], lw_ref[:, sl], k_ref[:, sl], v, a_ref[:, sl],
                             kk_ref[:, sl], ka_ref[:, sl], rk_ref[:, sl], roll, emit)

    _interleave(one_pair(p) for p in range(npairs))


def _rwkv_phase1(r, k, v, logw, a, v_first, vgate, k_k, k_a, r_k, *, pairs_per_step=4):
    T, D = r.shape
    C = CHUNK
    n_chunks = T // C
    n_pairs = D // LANES
    pp = min(pairs_per_step, n_pairs)
    assert n_pairs % pp == 0 and T % C == 0
    has_v = v_first is not None
    W = pp * LANES
    row_spec = pl.BlockSpec((C, W), lambda c, j: (c, j))
    par_spec = pl.BlockSpec((1, W), lambda c, j: (0, j))
    mat_spec = pl.BlockSpec((1, 2 * C, W), lambda c, j: (c, 0, j))
    ins = [r, k, v, logw, a] + ([v_first, vgate] if has_v else []) + [k_k, k_a, r_k]
    in_specs = [row_spec] * (7 if has_v else 5) + [par_spec] * 3
    mat_shape = jax.ShapeDtypeStruct((n_chunks, 2 * C, D), F32)
    row_shape = jax.ShapeDtypeStruct((T, D), F32)
    out_shape = [mat_shape, mat_shape, mat_shape, row_shape, row_shape]
    out_specs = [mat_spec, mat_spec, mat_spec, row_spec, row_spec]
    return pl.pallas_call(
        functools.partial(_rwkv_phase1_body, pp, has_v),
        out_shape=out_shape, grid=(n_chunks, n_pairs // pp), in_specs=in_specs, out_specs=out_specs,
        compiler_params=_cparams(2), name="rwkv_phase1",
    )(*ins)


def _rwkv_phase2_body(npairs, g_ref, m_ref, z_ref, yi_ref, bo_ref, lg_ref, lb_ref, y_ref, s_ref):
    @pl.when(pl.program_id(0) == 0)
    def _():
        s_ref[...] = jnp.zeros_like(s_ref)

    C = CHUNK
    lane = lax.broadcasted_iota(jnp.int32, (C, LANES), 1)
    lo = lane < RWKV_HEAD

    def one_pair(p):
        sl = slice(p * LANES, (p + 1) * LANES)
        s0 = s_ref[p]
        gm = jnp.concatenate([g_ref[0, :, sl], m_ref[0, :, sl]], axis=0)
        g_hi, g_lo = _split(gm)
        s_hi, s_lo = _split(s0)
        yield
        res = _dot(g_hi, s_hi) + _dot(g_hi, s_lo) + _dot(g_lo, s_hi)
        yield
        s_ref[p] = res[2 * C:] + z_ref[0, :, sl]
        yd = res[:2 * C]
        y = jnp.where(lo, yd[:C], yd[C:]) + yi_ref[:, sl]
        mean = _half_sum(y, lo) * (1.0 / RWKV_HEAD)
        yc = y - mean
        var = _half_sum(yc * yc, lo) * (1.0 / RWKV_HEAD)
        y_ref[:, sl] = yc * lax.rsqrt(var + GN_EPS) * lg_ref[:, sl] + lb_ref[:, sl] + bo_ref[:, sl]

    _interleave(one_pair(p) for p in range(npairs))


def _rwkv_phase2(g_bd, m_bd, z, yi, bonus, lnx_g, lnx_b):
    n_chunks, C2, D = g_bd.shape
    C = CHUNK
    T = n_chunks * C
    n_pairs = D // LANES
    row_spec = pl.BlockSpec((C, D), lambda c: (c, 0))
    par_spec = pl.BlockSpec((1, D), lambda c: (0, 0))
    mat_spec = pl.BlockSpec((1, 2 * C, D), lambda c: (c, 0, 0))
    return pl.pallas_call(
        functools.partial(_rwkv_phase2_body, n_pairs),
        out_shape=jax.ShapeDtypeStruct((T, D), F32),
        grid=(n_chunks,),
        in_specs=[mat_spec, mat_spec, mat_spec, row_spec, row_spec, par_spec, par_spec],
        out_specs=row_spec,
        scratch_shapes=[pltpu.VMEM((n_pairs, 2 * C, LANES), F32)],
        compiler_params=_cparams(1), name="rwkv_phase2",
    )(g_bd, m_bd, z, yi, bonus, lnx_g, lnx_b)


def _rwkv_layer(x, v_first, vres, mu, w_rkv, w0, w_a, w_b, a0, a_a, a_b, g_a, g_b, k_k, k_a, r_k,
                lnx_g, lnx_b, w_o, ln_g, ln_b, alpha):
    T, D = x.shape
    xp = jnp.concatenate([jnp.zeros((1, D), x.dtype), x[:-1]], axis=0)
    row = lambda t: t.reshape(1, -1)
    padc = lambda w: jnp.pad(w, ((0, 0), (0, (-w.shape[1]) % LANES))).astype(BF16)
    padr = lambda w: jnp.pad(w, ((0, (-w.shape[0]) % LANES), (0, 0))).astype(BF16)
    rkv = [
        _rowwise(_mix_proj_body, [x, xp], [row(mu[c]), w_rkv[c].astype(BF16)], [(D, F32)],
                 tm=256, name=f"rwkv_proj{c}")[0]
        for c in range(3)
    ]
    consts = [mu, row(w0), padc(w_a), padr(w_b), row(a0), padc(a_a), padr(a_b), padc(g_a), padr(g_b)]
    outs = [(D, F32)] * 3
    has_v = vres is not None
    if has_v:
        v0, v_a, v_b = vres
        consts += [row(v0), padc(v_a), padr(v_b)]
        outs = outs + [(D, F32)]
    lora = _rowwise(functools.partial(_lora_body, has_v), [x, xp], consts, outs, tm=256, name="rwkv_lora")
    logw, a, g = lora[:3]
    vgate = lora[3] if has_v else None
    ph1 = _rwkv_phase1(rkv[0], rkv[1], rkv[2], logw, a, v_first if has_v else None, vgate,
                       row(k_k), row(k_a), r_k.reshape(1, D))
    g_bd, m_bd, z, yi, bonus = ph1[:5]
    if not has_v:
        v_first = rkv[2]
    y = _rwkv_phase2(g_bd, m_bd, z, yi, bonus, row(lnx_g), row(lnx_b))
    x_new = _rowwise(functools.partial(_out_proj_ln_body, alpha), [y, g, x],
                     [w_o.astype(BF16), row(ln_g), row(ln_b)], [(D, F32)], tm=256, name="rwkv_out")[0]
    return x_new, v_first


def _fox_flash_body(tq, q_ref, k_ref, v_ref, o_ref, m_ref, l_ref, acc_ref):
    i = pl.program_id(1)
    m_ref[...] = jnp.full_like(m_ref, -1e30)
    l_ref[...] = jnp.zeros_like(l_ref)
    acc_ref[...] = jnp.zeros_like(acc_ref)

    def block(j, diagonal):
        start = pl.multiple_of(j * tq, tq)
        s = _dot(q_ref[...], k_ref[0, j])
        if diagonal:
            row = lax.broadcasted_iota(jnp.int32, (tq, tq), 0)
            col = lax.broadcasted_iota(jnp.int32, (tq, tq), 1)
            s = jnp.where(col <= row, s, -1e30)
        chunks = [s[:, c * LANES:(c + 1) * LANES] for c in range(tq // LANES)]
        m_loc = functools.reduce(jnp.maximum, chunks)
        m_old = m_ref[...]
        m_new = jnp.maximum(m_old, jnp.max(m_loc, axis=-1, keepdims=True))
        alpha = jnp.exp2(m_old - m_new)
        ps = [jnp.exp2(c - m_new) for c in chunks]
        l_ref[...] = alpha * l_ref[...] + functools.reduce(jnp.add, ps)
        p = jnp.concatenate(ps, axis=1).astype(BF16)
        acc_ref[...] = alpha * acc_ref[...] + _dot(p, v_ref[pl.ds(start, tq), :])
        m_ref[...] = m_new

    def pair(jj, carry):
        block(2 * jj, False)
        block(2 * jj + 1, False)
        return carry

    lax.fori_loop(0, i // 2, pair, 0)

    @pl.when(i % 2 == 1)
    def _():
        block(i - 1, False)

    block(i, True)
    o_ref[...] = (acc_ref[...] / jnp.sum(l_ref[...], axis=-1, keepdims=True)).astype(o_ref.dtype)


def _fox_flash(q_aug, k_aug, v, *, tq=512):
    T, D = v.shape
    H = D // FOX_HEAD
    tq = min(tq, T)
    k_t = k_aug.reshape(T // tq, tq, H, 2 * FOX_HEAD).transpose(2, 0, 3, 1)
    return pl.pallas_call(
        functools.partial(_fox_flash_body, tq),
        out_shape=jax.ShapeDtypeStruct((T, D), F32),
        grid=(H, T // tq),
        in_specs=[
            pl.BlockSpec((tq, 2 * FOX_HEAD), lambda h, i: (i, h)),
            pl.BlockSpec((1, T // tq, 2 * FOX_HEAD, tq), lambda h, i: (h, 0, 0, 0)),
            pl.BlockSpec((T, FOX_HEAD), lambda h, i: (0, h)),
        ],
        out_specs=pl.BlockSpec((tq, FOX_HEAD), lambda h, i: (i, h)),
        scratch_shapes=[pltpu.VMEM((tq, LANES), F32), pltpu.VMEM((tq, LANES), F32),
                        pltpu.VMEM((tq, FOX_HEAD), F32)],
        compiler_params=_cparams(2), name="fox_flash",
    )(q_aug, k_t, v)


def _fox_shared_kv(x, w_kvf, b_f, k_norm):
    T, D = x.shape
    H = D // FOX_HEAD
    w_f = jnp.pad(w_kvf[:, 2 * D:], ((0, 0), (0, LANES - H))).astype(BF16)
    b_pad = jnp.pad(b_f, (0, LANES - H)).reshape(1, LANES)
    c = _rowwise(_forget_cumsum_body, [x], [w_f, b_pad], [(LANES, F32)], tm=256, name="fox_forget",
                 scratch=[pltpu.VMEM((1, LANES), F32)])[0]
    k_aug = _rowwise(functools.partial(_proj_rms_aug_body, 1.0, False), [x, c],
                     [w_kvf[:, :D].astype(BF16), k_norm.reshape(1, FOX_HEAD)], [(2 * D, BF16)], tm=256,
                     name="fox_k")[0]
    v = _rowwise(_proj_body, [x], [w_kvf[:, D:2 * D].astype(BF16)], [(D, BF16)], tm=256, name="fox_v")[0]
    return k_aug, v, c


def _fox_layer(x, k_aug, v, c, w_qg, q_norm, w_o, ln_g, ln_b, alpha):
    T, D = x.shape
    row = lambda t: t.reshape(1, -1)
    q_aug = _rowwise(functools.partial(_proj_rms_aug_body, FOX_HEAD ** -0.5 * LOG2E, True), [x, c],
                     [w_qg[:, :D].astype(BF16), q_norm.reshape(1, FOX_HEAD)], [(2 * D, BF16)], tm=256,
                     name="fox_q")[0]
    gate = _rowwise(_proj_sigmoid_body, [x], [w_qg[:, D:].astype(BF16)], [(D, F32)], tm=256, name="fox_gate")[0]
    o = _fox_flash(q_aug, k_aug, v)
    return _rowwise(functools.partial(_out_proj_ln_body, alpha), [o, gate, x],
                    [w_o.astype(BF16), row(ln_g), row(ln_b)], [(D, F32)], tm=256, name="fox_out")[0]


def _router_body(x_ref, w_ref, b_ref, o_ref):
    logits = _dot3(x_ref[...], w_ref[...])
    scores = jax.nn.sigmoid(logits)
    biased = scores + b_ref[...]
    sc = [scores[:, e:e + 1] for e in range(N_EXPERTS)]
    bi = [biased[:, e:e + 1] for e in range(N_EXPERTS)]
    gs = []
    for g in range(N_GROUPS):
        a, b, c, d = bi[4 * g:4 * g + 4]
        hi1, lo1 = jnp.maximum(a, b), jnp.minimum(a, b)
        hi2, lo2 = jnp.maximum(c, d), jnp.minimum(c, d)
        gs.append(jnp.maximum(hi1, hi2) + jnp.maximum(jnp.minimum(hi1, hi2), jnp.maximum(lo1, lo2)))
    best, g_sel = gs[0], jnp.zeros_like(gs[0], dtype=jnp.int32)
    for g in range(1, N_GROUPS):
        better = gs[g] > best
        best = jnp.where(better, gs[g], best)
        g_sel = jnp.where(better, g, g_sel)

    def pick(cols, j):
        out = cols[j]
        for g in range(1, N_GROUPS):
            out = jnp.where(g_sel == g, cols[4 * g + j], out)
        return out

    ib = [pick(bi, j) for j in range(EXPERTS_PER_GROUP)]
    isc = [pick(sc, j) for j in range(EXPERTS_PER_GROUP)]

    def argmax4(vals):
        bv, bi_, bs = vals[0], jnp.zeros_like(g_sel), isc[0]
        for j in range(1, EXPERTS_PER_GROUP):
            better = vals[j] > bv
            bv = jnp.where(better, vals[j], bv)
            bi_ = jnp.where(better, j, bi_)
            bs = jnp.where(better, isc[j], bs)
        return bi_, bs

    i1, s1 = argmax4(ib)
    ib2 = [jnp.where(i1 == j, -jnp.inf, ib[j]) for j in range(EXPERTS_PER_GROUP)]
    i2, s2 = argmax4(ib2)
    den = s1 + s2
    e1 = (g_sel * EXPERTS_PER_GROUP + i1).astype(F32)
    e2 = (g_sel * EXPERTS_PER_GROUP + i2).astype(F32)
    lane = lax.broadcasted_iota(jnp.int32, o_ref.shape, 1)
    o_ref[...] = jnp.where(lane == 0, e1, jnp.where(lane == 1, e2, jnp.where(lane == 2, s1 / den, s2 / den)))


def _moe_expert_body(tm, tile_e_ref, src_ref, nt_ref, x_hbm, win_ref, wout_ref, rw_ref, o_ref, xbuf, sem):
    i = pl.program_id(0)

    @pl.when(i < nt_ref[0])
    def _():
        def row_copy(r):
            tok = src_ref[i * tm + r]
            return pltpu.make_async_copy(x_hbm.at[pl.ds(tok, 1), :], xbuf.at[pl.ds(r, 1), :], sem.at[0])

        def issue(r, carry):
            row_copy(r).start()
            return carry

        lax.fori_loop(0, tm, issue, 0)

        def drain(r, carry):
            row_copy(r).wait()
            return carry

        lax.fori_loop(0, tm, drain, 0)
        F = wout_ref.shape[1]
        h = _dot(xbuf[...].astype(BF16), win_ref[0])
        act = h[:, :F] * jax.nn.silu(h[:, F:])
        o_ref[...] = _dot(act.astype(BF16), wout_ref[0]) * rw_ref[...]

    @pl.when(i >= nt_ref[0])
    def _():
        o_ref[...] = jnp.zeros_like(o_ref)


def _moe_combine_body(tm, alpha, pos_ref, y_hbm, x_ref, lg_ref, lb_ref, o_ref, buf, sem):
    i = pl.program_id(0)

    def row_copy(r, j):
        p = pos_ref[2 * (i * tm + r) + j]
        return pltpu.make_async_copy(y_hbm.at[pl.ds(p, 1), :], buf.at[j, pl.ds(r, 1), :], sem.at[j])

    def issue(r, carry):
        row_copy(r, 0).start()
        row_copy(r, 1).start()
        return carry

    lax.fori_loop(0, tm, issue, 0)

    def drain(r, carry):
        row_copy(r, 0).wait()
        row_copy(r, 1).wait()
        return carry

    lax.fori_loop(0, tm, drain, 0)
    z = alpha * x_ref[...] + (buf[0] + buf[1])
    o_ref[...] = _layer_norm(z, lg_ref[...], lb_ref[...])


def _moe_layer(x, router_w, router_b, w_in, w_out, ln_g, ln_b, alpha, *, tm=256):
    T, D = x.shape
    E, F = w_out.shape[0], w_out.shape[1]
    tm = min(tm, T)
    rw_pad = jnp.pad(router_w, ((0, 0), (0, LANES - E)))
    rb_pad = jnp.pad(router_b, (0, LANES - E)).reshape(1, LANES)
    route = _rowwise(_router_body, [x], [rw_pad, rb_pad], [(LANES, F32)], tm=256, name="moe_router")[0]
    e_flat = route[:, :2].astype(jnp.int32).reshape(-1)
    w_flat = route[:, 2:4].reshape(-1)
    onehot = (e_flat[:, None] == jnp.arange(E)[None, :]).astype(jnp.int32)
    rank = jnp.take_along_axis(jnp.cumsum(onehot, axis=0) - onehot, e_flat[:, None], axis=1)[:, 0]
    counts = jnp.sum(onehot, axis=0)
    padded = ((counts + tm - 1) // tm) * tm
    ends = jnp.cumsum(padded)
    starts = ends - padded
    dest = starts[e_flat] + rank
    n_rows = 2 * T + E * tm
    n_tiles = n_rows // tm
    src_tok = jnp.zeros((n_rows,), jnp.int32).at[dest].set(jnp.arange(2 * T, dtype=jnp.int32) // 2)
    row_w = jnp.zeros((n_rows, 1), F32).at[dest, 0].set(w_flat)
    used_tiles = (ends[-1] // tm).astype(jnp.int32)
    tile_start = jnp.minimum(jnp.arange(n_tiles, dtype=jnp.int32), used_tiles - 1) * tm
    tile_e = jnp.minimum(jnp.searchsorted(ends, tile_start, side="right"), E - 1).astype(jnp.int32)

    y_sorted = pl.pallas_call(
        functools.partial(_moe_expert_body, tm),
        out_shape=jax.ShapeDtypeStruct((n_rows, D), F32),
        grid_spec=pltpu.PrefetchScalarGridSpec(
            num_scalar_prefetch=3, grid=(n_tiles,),
            in_specs=[
                pl.BlockSpec(memory_space=pl.ANY),
                pl.BlockSpec((1, D, 2 * F), lambda i, te, src, nt: (te[i], 0, 0)),
                pl.BlockSpec((1, F, D), lambda i, te, src, nt: (te[i], 0, 0)),
                pl.BlockSpec((tm, 1), lambda i, te, src, nt: (i, 0)),
            ],
            out_specs=pl.BlockSpec((tm, D), lambda i, te, src, nt: (i, 0)),
            scratch_shapes=[pltpu.VMEM((tm, D), F32), pltpu.SemaphoreType.DMA((1,))],
        ),
        compiler_params=_cparams(1), name="moe_experts",
    )(tile_e, src_tok, used_tiles.reshape(1), x, w_in.astype(BF16), w_out.astype(BF16), row_w)

    row = lambda t: t.reshape(1, -1)
    return pl.pallas_call(
        functools.partial(_moe_combine_body, tm, alpha),
        out_shape=jax.ShapeDtypeStruct((T, D), F32),
        grid_spec=pltpu.PrefetchScalarGridSpec(
            num_scalar_prefetch=1, grid=(T // tm,),
            in_specs=[
                pl.BlockSpec(memory_space=pl.ANY),
                pl.BlockSpec((tm, D), lambda i, pos: (i, 0)),
                pl.BlockSpec((1, D), lambda i, pos: (0, 0)),
                pl.BlockSpec((1, D), lambda i, pos: (0, 0)),
            ],
            out_specs=pl.BlockSpec((tm, D), lambda i, pos: (i, 0)),
            scratch_shapes=[pltpu.VMEM((2, tm, D), F32), pltpu.SemaphoreType.DMA((2,))],
        ),
        compiler_params=_cparams(1), name="moe_combine",
    )(dest.astype(jnp.int32), y_sorted, x, row(ln_g), row(ln_b))


def kernel(x, rw_mu, rw_w_rkv, rw_w0, rw_w_a, rw_w_b, rw_a0, rw_a_a, rw_a_b, rw_v0, rw_v_a, rw_v_b, rw_g_a, rw_g_b, rw_k_k, rw_k_a, rw_r_k, rw_lnx_g, rw_lnx_b, rw_w_o, fx_w_kvf, fx_b_f, fx_k_norm, fx_w_qg, fx_q_norm, fx_w_o, router_w, router_b, moe_w_in, moe_w_out, ln_g, ln_b):
    B, T, D = x.shape
    depth = ln_g.shape[0]
    n_a = rw_mu.shape[0]
    alpha = (2 * depth) ** 0.25
    outs = []
    for bi in range(B):
        h = x[bi]
        v_first = None
        kv = None
        for l in range(depth):
            if l < n_a:
                vres = None if l == 0 else (rw_v0[l - 1], rw_v_a[l - 1], rw_v_b[l - 1])
                h, v_first = _rwkv_layer(
                    h, v_first, vres, rw_mu[l], rw_w_rkv[l], rw_w0[l], rw_w_a[l], rw_w_b[l], rw_a0[l],
                    rw_a_a[l], rw_a_b[l], rw_g_a[l], rw_g_b[l], rw_k_k[l], rw_k_a[l], rw_r_k[l],
                    rw_lnx_g[l], rw_lnx_b[l], rw_w_o[l], ln_g[l, 0], ln_b[l, 0], alpha)
            else:
                if kv is None:
                    kv = _fox_shared_kv(h, fx_w_kvf, fx_b_f, fx_k_norm)
                j = l - n_a
                h = _fox_layer(h, *kv, fx_w_qg[j], fx_q_norm[j], fx_w_o[j], ln_g[l, 0], ln_b[l, 0], alpha)
            h = _moe_layer(h, router_w, router_b, moe_w_in[l], moe_w_out[l], ln_g[l, 1], ln_b[l, 1], alpha)
        outs.append(h)
    return jnp.stack(outs, axis=0)
```

```python
import functools

import jax
import jax.numpy as jnp
from jax import lax
from jax.experimental import pallas as pl
from jax.experimental.pallas import tpu as pltpu

F32 = jnp.float32
BF16 = jnp.bfloat16

RWKV_HEAD = 64
FOX_HEAD = 128
N_EXPERTS = 16
N_GROUPS = 4
EXPERTS_PER_GROUP = N_EXPERTS // N_GROUPS
GN_EPS = 64e-5
LN_EPS = 1e-5
RMS_EPS = 1e-6
CHUNK = 64
PHASE1_PAIRS_PER_STEP = 8
LANES = 128
LOG2E = 1.4426950408889634
VMEM_LIMIT = 56 * 1024 * 1024


def _cparams(n_axes, vmem=VMEM_LIMIT):
    return pltpu.CompilerParams(dimension_semantics=("arbitrary",) * n_axes, vmem_limit_bytes=vmem)


def _dot(a, b):
    return jnp.dot(a, b, preferred_element_type=F32)


def _dot_nt(a, b):
    return lax.dot_general(a, b, (((1,), (1,)), ((), ())), preferred_element_type=F32)


def _split(a):
    hi = a.astype(BF16)
    lo = (a - hi.astype(F32)).astype(BF16)
    return hi, lo


def _dot3(a, b):
    ah, al = _split(a)
    bh, bl = _split(b)
    return _dot(ah, bh) + _dot(ah, bl) + _dot(al, bh)


def _layer_norm(z, g, b):
    mu = jnp.mean(z, axis=-1, keepdims=True)
    zc = z - mu
    var = jnp.mean(zc * zc, axis=-1, keepdims=True)
    return zc * lax.rsqrt(var + LN_EPS) * g + b


def _rowwise(body, row_args, const_args, outs, *, tm, name, scratch=()):
    T = row_args[0].shape[0]
    tm = min(tm, T)
    assert T % tm == 0
    in_specs = [pl.BlockSpec((tm, a.shape[1]), lambda i: (i, 0)) for a in row_args]
    in_specs += [pl.BlockSpec(a.shape, lambda i, nd=a.ndim: (0,) * nd) for a in const_args]
    out_shape = [jax.ShapeDtypeStruct((T, w), dt) for (w, dt) in outs]
    out_specs = [pl.BlockSpec((tm, w), lambda i: (i, 0)) for (w, dt) in outs]
    res = pl.pallas_call(
        body, out_shape=out_shape, grid=(T // tm,), in_specs=in_specs, out_specs=out_specs,
        scratch_shapes=list(scratch), compiler_params=_cparams(1), name=name,
    )(*row_args, *const_args)
    return res


def _mix_proj_body(x_ref, xp_ref, mu_ref, w_ref, o_ref):
    x = x_ref[...]
    xs = x + (xp_ref[...] - x) * mu_ref[...]
    o_ref[...] = _dot(xs.astype(BF16), w_ref[...])


def _lora_body(has_v, x_ref, xp_ref, mu_ref, w0_ref, wa_ref, wb_ref, a0_ref, aa_ref, ab_ref,
               ga_ref, gb_ref, *rest):
    if has_v:
        v0_ref, va_ref, vb_ref, logw_ref, a_ref, g_ref, vg_ref = rest
    else:
        logw_ref, a_ref, g_ref = rest
    x = x_ref[...]
    dx = xp_ref[...] - x
    mix = lambda c: (x + dx * mu_ref[c:c + 1, :]).astype(BF16)
    w_raw = w0_ref[...] + _dot(jnp.tanh(_dot(mix(3), wa_ref[...])).astype(BF16), wb_ref[...])
    logw_ref[...] = (-0.6065306597126334) * jax.nn.sigmoid(w_raw)
    a_ref[...] = jax.nn.sigmoid(a0_ref[...] + _dot(_dot(mix(4), aa_ref[...]).astype(BF16), ab_ref[...]))
    g_ref[...] = _dot(jax.nn.sigmoid(_dot(mix(5), ga_ref[...])).astype(BF16), gb_ref[...])
    if has_v:
        vg_ref[...] = jax.nn.sigmoid(v0_ref[...] + _dot(_dot(mix(2), va_ref[...]).astype(BF16), vb_ref[...]))


def _out_proj_ln_body(alpha, y_ref, g_ref, x_ref, w_ref, lg_ref, lb_ref, o_ref):
    h = _dot((y_ref[...].astype(F32) * g_ref[...].astype(F32)).astype(BF16), w_ref[...])
    o_ref[...] = _layer_norm(alpha * x_ref[...] + h, lg_ref[...], lb_ref[...])


def _proj_rms_aug_body(scale, is_query, x_ref, c_ref, w_ref, gain_ref, o_ref):
    acc = _dot(x_ref[...].astype(BF16), w_ref[...])
    c2 = c_ref[...] * LOG2E
    lane = lax.broadcasted_iota(jnp.int32, c2.shape, 1)
    for h in range(acc.shape[1] // FOX_HEAD):
        t = acc[:, h * FOX_HEAD:(h + 1) * FOX_HEAD]
        ms = jnp.mean(t * t, axis=-1, keepdims=True)
        base = 2 * h * FOX_HEAD
        o_ref[:, base:base + FOX_HEAD] = (t * lax.rsqrt(ms + RMS_EPS) * gain_ref[...] * scale).astype(o_ref.dtype)
        ch = jnp.sum(jnp.where(lane == h, c2, 0.0), axis=-1, keepdims=True)
        if not is_query:
            ch = -ch
        hi = ch.astype(BF16).astype(F32)
        mid = (ch - hi).astype(BF16).astype(F32)
        low = ch - hi - mid
        t0 = 0 if is_query else 3
        one = ((lane >= 3 - t0) & (lane < 6 - t0)).astype(F32)
        aug = jnp.where(lane == t0, hi, jnp.where(lane == t0 + 1, mid, jnp.where(lane == t0 + 2, low, one)))
        o_ref[:, base + FOX_HEAD:base + 2 * FOX_HEAD] = aug.astype(o_ref.dtype)


def _proj_body(x_ref, w_ref, o_ref):
    o_ref[...] = _dot(x_ref[...].astype(BF16), w_ref[...]).astype(o_ref.dtype)


def _proj_sigmoid_body(x_ref, w_ref, o_ref):
    o_ref[...] = jax.nn.sigmoid(_dot(x_ref[...].astype(BF16), w_ref[...])).astype(o_ref.dtype)


def _log_sigmoid(z):
    return jnp.minimum(z, 0.0) - jnp.log(1.0 + jnp.exp(-jnp.abs(z)))


def _forget_cumsum_body(x_ref, w_ref, b_ref, c_ref, carry_ref):
    @pl.when(pl.program_id(0) == 0)
    def _():
        carry_ref[...] = jnp.zeros_like(carry_ref)

    logf = _log_sigmoid(_dot(x_ref[...].astype(BF16), w_ref[...]) + b_ref[...])
    tm = logf.shape[0]
    tri = (lax.broadcasted_iota(jnp.int32, (tm, tm), 0) >= lax.broadcasted_iota(jnp.int32, (tm, tm), 1))
    tri = tri.astype(BF16)
    hi = logf.astype(BF16)
    r1 = logf - hi.astype(F32)
    mid = r1.astype(BF16)
    lo = (r1 - mid.astype(F32)).astype(BF16)
    c = _dot(tri, hi) + _dot(tri, mid) + _dot(tri, lo) + carry_ref[...]
    c_ref[...] = c
    carry_ref[...] = c[tm - 1:tm, :]


def _half_sum(x, lo):
    s_lo = jnp.sum(jnp.where(lo, x, 0.0), axis=-1, keepdims=True)
    s_hi = jnp.sum(jnp.where(lo, 0.0, x), axis=-1, keepdims=True)
    return jnp.where(lo, s_lo, s_hi)


def _to_dd(x, lo, roll):
    xr = roll(x, RWKV_HEAD, 1)
    return jnp.concatenate([jnp.where(lo, x, xr), jnp.where(lo, xr, x)], axis=0)


def _interleave(gens):
    gens = list(gens)
    while gens:
        alive = []
        for gen in gens:
            try:
                next(gen)
                alive.append(gen)
            except StopIteration:
                pass
        gens = alive


def _chunk_phase1(r, logw, k_raw, v, a, k_k, k_a, r_k, roll, emit):
    C = r.shape[0]
    lane = lax.broadcasted_iota(jnp.int32, (C, LANES), 1)
    lo = lane < RWKV_HEAD
    kk = k_raw * k_k
    kk = kk / jnp.maximum(jnp.sqrt(_half_sum(kk * kk, lo)), 1e-12)
    k = k_raw * (1.0 + (a - 1.0) * k_a)
    bonus = _half_sum(r * k * r_k, lo) * v
    b = kk * a
    row = lax.broadcasted_iota(jnp.int32, (C, LANES), 0)
    cw = logw
    s = 1
    while s < C:
        cw = cw + jnp.where(row >= s, roll(cw, s, 0), 0.0)
        s *= 2
    cw_end = cw[C - 1:C, :]
    g_in = jnp.exp(cw)
    g_ex = jnp.exp(cw - logw)
    g_inv = jnp.exp(-cw)
    g_rem = jnp.exp(cw_end - cw)
    dd = lambda x: _to_dd(x, lo, roll)
    a_t = dd(-kk * g_ex)
    r_t = dd(r * g_in)
    k_h = dd(k * g_inv)
    b_h = dd(b * g_inv)
    k_g = dd(k * g_rem)
    b_g = dd(b * g_rem)
    v_d = dd(v)
    ri = lax.broadcasted_iota(jnp.int32, (2 * C, 2 * C), 0)
    ci = lax.broadcasted_iota(jnp.int32, (2 * C, 2 * C), 1)
    same = (ri < C) == (ci < C)
    strict = same & (ri > ci)
    incl = same & (ri >= ci)
    ar = jnp.concatenate([a_t, r_t], axis=0).astype(BF16)
    kb = jnp.concatenate([k_h, b_h], axis=0).astype(BF16)
    yield
    prod = 0.5 * _dot_nt(ar, kb)
    yield
    a_ak = jnp.where(strict, prod[:2 * C, :2 * C], 0.0)
    a_ab = jnp.where(strict, prod[:2 * C, 2 * C:], 0.0)
    a_rk = jnp.where(incl, prod[2 * C:, :2 * C], 0.0)
    a_rb = jnp.where(incl, prod[2 * C:, 2 * C:], 0.0)
    v16 = v_d.astype(BF16)
    n16 = a_ab.astype(BF16)
    npow = _dot(n16, n16)
    akv = _dot(jnp.concatenate([a_ak, a_rk], axis=0).astype(BF16), v16)
    yield
    tinv = (ri == ci).astype(F32) + a_ab
    s = 4
    while s < C:
        both = _dot(npow.astype(BF16), jnp.concatenate([tinv, npow], axis=1).astype(BF16))
        yield
        tinv = tinv + both[:, :2 * C]
        npow = both[:, 2 * C:]
        s *= 2
    tinv = tinv + _dot(npow.astype(BF16), tinv.astype(BF16))
    yield
    pq = _dot(tinv.astype(BF16), jnp.concatenate([a_t, akv[:2 * C]], axis=1).astype(BF16))
    yield
    pq16 = pq.astype(BF16)
    rbpq = _dot(a_rb.astype(BF16), pq16)
    kbg_t = jnp.concatenate([jnp.where(same, k_g, 0.0), jnp.where(same, b_g, 0.0)], axis=0).T
    rhs = jnp.concatenate([jnp.concatenate([v16, jnp.zeros_like(v16)], axis=1),
                           jnp.concatenate([pq16[:, LANES:], pq16[:, :LANES]], axis=1)], axis=0)
    zm = _dot(kbg_t.astype(BF16), rhs)
    yield
    g = r_t + rbpq[:, :LANES]
    yi = akv[2 * C:] + rbpq[:, LANES:]
    m_bd = jnp.where(same, zm[:, LANES:], 0.0) + jnp.where(ri == ci, jnp.exp(cw_end), 0.0)
    emit(jnp.where(same, g, 0.0), m_bd, zm[:, :LANES], jnp.where(lo, yi[:C], yi[C:]), bonus)


def _rwkv_phase1_body(npairs, has_v, r_ref, k_ref, v_ref, lw_ref, a_ref, *rest):
    if has_v:
        vf_ref, vg_ref, kk_ref, ka_ref, rk_ref, g_ref, m_ref, z_ref, yi_ref, bo_ref = rest
    else:
        kk_ref, ka_ref, rk_ref, g_ref, m_ref, z_ref, yi_ref, bo_ref = rest
    roll = lambda x, s, ax: pltpu.roll(x, s, ax)

    def one_pair(p):
        sl = slice(p * LANES, (p + 1) * LANES)
        v = v_ref[:, sl]
        if has_v:
            v = v + (vf_ref[:, sl] - v) * vg_ref[:, sl]

        def emit(g_bd, m_bd, z, yi, bonus):
            g_ref[0, :, sl] = g_bd
            m_ref[0, :, sl] = m_bd
            z_ref[0, :, sl] = z
            yi_ref[:, sl] = yi
            bo_ref[:, sl] = bonus

        return _chunk_phase1(r_ref[:, sl], lw_ref[:, sl], k_ref[:, sl], v, a_ref[:, sl],
                             kk_ref[:, sl], ka_ref[:, sl], rk_ref[:, sl], roll, emit)

    _interleave(one_pair(p) for p in range(npairs))


def _rwkv_phase1(r, k, v, logw, a, v_first, vgate, k_k, k_a, r_k, *, pairs_per_step=PHASE1_PAIRS_PER_STEP):
    T, D = r.shape
    C = CHUNK
    n_chunks = T // C
    n_pairs = D // LANES
    pp = min(pairs_per_step, n_pairs)
    assert n_pairs % pp == 0 and T % C == 0
    has_v = v_first is not None
    W = pp * LANES
    row_spec = pl.BlockSpec((C, W), lambda c, j: (c, j))
    par_spec = pl.BlockSpec((1, W), lambda c, j: (0, j))
    mat_spec = pl.BlockSpec((1, 2 * C, W), lambda c, j: (c, 0, j))
    ins = [r, k, v, logw, a] + ([v_first, vgate] if has_v else []) + [k_k, k_a, r_k]
    in_specs = [row_spec] * (7 if has_v else 5) + [par_spec] * 3
    mat_shape = jax.ShapeDtypeStruct((n_chunks, 2 * C, D), F32)
    row_shape = jax.ShapeDtypeStruct((T, D), F32)
    out_shape = [mat_shape, mat_shape, mat_shape, row_shape, row_shape]
    out_specs = [mat_spec, mat_spec, mat_spec, row_spec, row_spec]
    return pl.pallas_call(
        functools.partial(_rwkv_phase1_body, pp, has_v),
        out_shape=out_shape, grid=(n_chunks, n_pairs // pp), in_specs=in_specs, out_specs=out_specs,
        compiler_params=_cparams(2), name="rwkv_phase1",
    )(*ins)


def _rwkv_phase2_body(npairs, g_ref, m_ref, z_ref, yi_ref, bo_ref, lg_ref, lb_ref, y_ref, s_ref):
    @pl.when(pl.program_id(0) == 0)
    def _():
        s_ref[...] = jnp.zeros_like(s_ref)

    C = CHUNK
    lane = lax.broadcasted_iota(jnp.int32, (C, LANES), 1)
    lo = lane < RWKV_HEAD

    def one_pair(p):
        sl = slice(p * LANES, (p + 1) * LANES)
        s0 = s_ref[p]
        gm = jnp.concatenate([g_ref[0, :, sl], m_ref[0, :, sl]], axis=0)
        g_hi, g_lo = _split(gm)
        s_hi, s_lo = _split(s0)
        yield
        res = _dot(g_hi, s_hi) + _dot(g_hi, s_lo) + _dot(g_lo, s_hi)
        yield
        s_ref[p] = res[2 * C:] + z_ref[0, :, sl]
        yd = res[:2 * C]
        y = jnp.where(lo, yd[:C], yd[C:]) + yi_ref[:, sl]
        mean = _half_sum(y, lo) * (1.0 / RWKV_HEAD)
        yc = y - mean
        var = _half_sum(yc * yc, lo) * (1.0 / RWKV_HEAD)
        y_ref[:, sl] = yc * lax.rsqrt(var + GN_EPS) * lg_ref[:, sl] + lb_ref[:, sl] + bo_ref[:, sl]

    _interleave(one_pair(p) for p in range(npairs))


def _rwkv_phase2(g_bd, m_bd, z, yi, bonus, lnx_g, lnx_b):
    n_chunks, C2, D = g_bd.shape
    C = CHUNK
    T = n_chunks * C
    n_pairs = D // LANES
    row_spec = pl.BlockSpec((C, D), lambda c: (c, 0))
    par_spec = pl.BlockSpec((1, D), lambda c: (0, 0))
    mat_spec = pl.BlockSpec((1, 2 * C, D), lambda c: (c, 0, 0))
    return pl.pallas_call(
        functools.partial(_rwkv_phase2_body, n_pairs),
        out_shape=jax.ShapeDtypeStruct((T, D), F32),
        grid=(n_chunks,),
        in_specs=[mat_spec, mat_spec, mat_spec, row_spec, row_spec, par_spec, par_spec],
        out_specs=row_spec,
        scratch_shapes=[pltpu.VMEM((n_pairs, 2 * C, LANES), F32)],
        compiler_params=_cparams(1), name="rwkv_phase2",
    )(g_bd, m_bd, z, yi, bonus, lnx_g, lnx_b)


def _rwkv_layer(x, v_first, vres, mu, w_rkv, w0, w_a, w_b, a0, a_a, a_b, g_a, g_b, k_k, k_a, r_k,
                lnx_g, lnx_b, w_o, ln_g, ln_b, alpha):
    T, D = x.shape
    xp = jnp.concatenate([jnp.zeros((1, D), x.dtype), x[:-1]], axis=0)
    row = lambda t: t.reshape(1, -1)
    padc = lambda w: jnp.pad(w, ((0, 0), (0, (-w.shape[1]) % LANES))).astype(BF16)
    padr = lambda w: jnp.pad(w, ((0, (-w.shape[0]) % LANES), (0, 0))).astype(BF16)
    rkv = [
        _rowwise(_mix_proj_body, [x, xp], [row(mu[c]), w_rkv[c].astype(BF16)], [(D, F32)],
                 tm=256, name=f"rwkv_proj{c}")[0]
        for c in range(3)
    ]
    consts = [mu, row(w0), padc(w_a), padr(w_b), row(a0), padc(a_a), padr(a_b), padc(g_a), padr(g_b)]
    outs = [(D, F32)] * 3
    has_v = vres is not None
    if has_v:
        v0, v_a, v_b = vres
        consts += [row(v0), padc(v_a), padr(v_b)]
        outs = outs + [(D, F32)]
    lora = _rowwise(functools.partial(_lora_body, has_v), [x, xp], consts, outs, tm=256, name="rwkv_lora")
    logw, a, g = lora[:3]
    vgate = lora[3] if has_v else None
    ph1 = _rwkv_phase1(rkv[0], rkv[1], rkv[2], logw, a, v_first if has_v else None, vgate,
                       row(k_k), row(k_a), r_k.reshape(1, D))
    g_bd, m_bd, z, yi, bonus = ph1[:5]
    if not has_v:
        v_first = rkv[2]
    y = _rwkv_phase2(g_bd, m_bd, z, yi, bonus, row(lnx_g), row(lnx_b))
    x_new = _rowwise(functools.partial(_out_proj_ln_body, alpha), [y, g, x],
                     [w_o.astype(BF16), row(ln_g), row(ln_b)], [(D, F32)], tm=256, name="rwkv_out")[0]
    return x_new, v_first


def _fox_flash_body(tq, q_ref, k_ref, v_ref, o_ref, m_ref, l_ref, acc_ref):
    i = pl.program_id(1)
    m_ref[...] = jnp.full_like(m_ref, -1e30)
    l_ref[...] = jnp.zeros_like(l_ref)
    acc_ref[...] = jnp.zeros_like(acc_ref)

    def block(j, diagonal):
        start = pl.multiple_of(j * tq, tq)
        s = _dot(q_ref[...], k_ref[0, j])
        if diagonal:
            row = lax.broadcasted_iota(jnp.int32, (tq, tq), 0)
            col = lax.broadcasted_iota(jnp.int32, (tq, tq), 1)
            s = jnp.where(col <= row, s, -1e30)
        chunks = [s[:, c * LANES:(c + 1) * LANES] for c in range(tq // LANES)]
        m_loc = functools.reduce(jnp.maximum, chunks)
        m_old = m_ref[...]
        m_new = jnp.maximum(m_old, jnp.max(m_loc, axis=-1, keepdims=True))
        alpha = jnp.exp2(m_old - m_new)
        ps = [jnp.exp2(c - m_new) for c in chunks]
        l_ref[...] = alpha * l_ref[...] + functools.reduce(jnp.add, ps)
        p = jnp.concatenate(ps, axis=1).astype(BF16)
        acc_ref[...] = alpha * acc_ref[...] + _dot(p, v_ref[pl.ds(start, tq), :])
        m_ref[...] = m_new

    def pair(jj, carry):
        block(2 * jj, False)
        block(2 * jj + 1, False)
        return carry

    lax.fori_loop(0, i // 2, pair, 0)

    @pl.when(i % 2 == 1)
    def _():
        block(i - 1, False)

    block(i, True)
    o_ref[...] = (acc_ref[...] / jnp.sum(l_ref[...], axis=-1, keepdims=True)).astype(o_ref.dtype)


def _fox_flash(q_aug, k_aug, v, *, tq=512):
    T, D = v.shape
    H = D // FOX_HEAD
    tq = min(tq, T)
    k_t = k_aug.reshape(T // tq, tq, H, 2 * FOX_HEAD).transpose(2, 0, 3, 1)
    return pl.pallas_call(
        functools.partial(_fox_flash_body, tq),
        out_shape=jax.ShapeDtypeStruct((T, D), F32),
        grid=(H, T // tq),
        in_specs=[
            pl.BlockSpec((tq, 2 * FOX_HEAD), lambda h, i: (i, h)),
            pl.BlockSpec((1, T // tq, 2 * FOX_HEAD, tq), lambda h, i: (h, 0, 0, 0)),
            pl.BlockSpec((T, FOX_HEAD), lambda h, i: (0, h)),
        ],
        out_specs=pl.BlockSpec((tq, FOX_HEAD), lambda h, i: (i, h)),
        scratch_shapes=[pltpu.VMEM((tq, LANES), F32), pltpu.VMEM((tq, LANES), F32),
                        pltpu.VMEM((tq, FOX_HEAD), F32)],
        compiler_params=_cparams(2), name="fox_flash",
    )(q_aug, k_t, v)


def _fox_shared_kv(x, w_kvf, b_f, k_norm):
    T, D = x.shape
    H = D // FOX_HEAD
    w_f = jnp.pad(w_kvf[:, 2 * D:], ((0, 0), (0, LANES - H))).astype(BF16)
    b_pad = jnp.pad(b_f, (0, LANES - H)).reshape(1, LANES)
    c = _rowwise(_forget_cumsum_body, [x], [w_f, b_pad], [(LANES, F32)], tm=256, name="fox_forget",
                 scratch=[pltpu.VMEM((1, LANES), F32)])[0]
    k_aug = _rowwise(functools.partial(_proj_rms_aug_body, 1.0, False), [x, c],
                     [w_kvf[:, :D].astype(BF16), k_norm.reshape(1, FOX_HEAD)], [(2 * D, BF16)], tm=256,
                     name="fox_k")[0]
    v = _rowwise(_proj_body, [x], [w_kvf[:, D:2 * D].astype(BF16)], [(D, BF16)], tm=256, name="fox_v")[0]
    return k_aug, v, c


def _fox_layer(x, k_aug, v, c, w_qg, q_norm, w_o, ln_g, ln_b, alpha):
    T, D = x.shape
    row = lambda t: t.reshape(1, -1)
    q_aug = _rowwise(functools.partial(_proj_rms_aug_body, FOX_HEAD ** -0.5 * LOG2E, True), [x, c],
                     [w_qg[:, :D].astype(BF16), q_norm.reshape(1, FOX_HEAD)], [(2 * D, BF16)], tm=256,
                     name="fox_q")[0]
    gate = _rowwise(_proj_sigmoid_body, [x], [w_qg[:, D:].astype(BF16)], [(D, F32)], tm=256, name="fox_gate")[0]
    o = _fox_flash(q_aug, k_aug, v)
    return _rowwise(functools.partial(_out_proj_ln_body, alpha), [o, gate, x],
                    [w_o.astype(BF16), row(ln_g), row(ln_b)], [(D, F32)], tm=256, name="fox_out")[0]


def _router_body(x_ref, w_ref, b_ref, o_ref, cnt_ref, cnt_acc):
    @pl.when(pl.program_id(0) == 0)
    def _():
        cnt_acc[...] = jnp.zeros_like(cnt_acc)

    logits = _dot3(x_ref[...], w_ref[...])
    scores = jax.nn.sigmoid(logits)
    biased = scores + b_ref[...]
    sc = [scores[:, e:e + 1] for e in range(N_EXPERTS)]
    bi = [biased[:, e:e + 1] for e in range(N_EXPERTS)]
    gs = []
    for g in range(N_GROUPS):
        a, b, c, d = bi[4 * g:4 * g + 4]
        hi1, lo1 = jnp.maximum(a, b), jnp.minimum(a, b)
        hi2, lo2 = jnp.maximum(c, d), jnp.minimum(c, d)
        gs.append(jnp.maximum(hi1, hi2) + jnp.maximum(jnp.minimum(hi1, hi2), jnp.maximum(lo1, lo2)))
    best, g_sel = gs[0], jnp.zeros_like(gs[0], dtype=jnp.int32)
    for g in range(1, N_GROUPS):
        better = gs[g] > best
        best = jnp.where(better, gs[g], best)
        g_sel = jnp.where(better, g, g_sel)

    def pick(cols, j):
        out = cols[j]
        for g in range(1, N_GROUPS):
            out = jnp.where(g_sel == g, cols[4 * g + j], out)
        return out

    ib = [pick(bi, j) for j in range(EXPERTS_PER_GROUP)]
    isc = [pick(sc, j) for j in range(EXPERTS_PER_GROUP)]

    def argmax4(vals):
        bv, bi_, bs = vals[0], jnp.zeros_like(g_sel), isc[0]
        for j in range(1, EXPERTS_PER_GROUP):
            better = vals[j] > bv
            bv = jnp.where(better, vals[j], bv)
            bi_ = jnp.where(better, j, bi_)
            bs = jnp.where(better, isc[j], bs)
        return bi_, bs

    i1, s1 = argmax4(ib)
    ib2 = [jnp.where(i1 == j, -jnp.inf, ib[j]) for j in range(EXPERTS_PER_GROUP)]
    i2, s2 = argmax4(ib2)
    den = s1 + s2
    e1 = g_sel * EXPERTS_PER_GROUP + i1
    e2 = g_sel * EXPERTS_PER_GROUP + i2
    tm = logits.shape[0]
    lane = lax.broadcasted_iota(jnp.int32, (tm, LANES), 1)
    oh1 = lane == e1
    oh2 = lane == e2
    oh = oh1 | oh2
    before = lax.broadcasted_iota(jnp.int32, (tm, tm), 0) > lax.broadcasted_iota(jnp.int32, (tm, tm), 1)
    prefix = _dot(before.astype(BF16), oh.astype(BF16)) + cnt_acc[...]
    rank1 = jnp.sum(jnp.where(oh1, prefix, 0.0), axis=-1, keepdims=True)
    rank2 = jnp.sum(jnp.where(oh2, prefix, 0.0), axis=-1, keepdims=True)
    cnt_acc[...] = cnt_acc[...] + jnp.sum(oh.astype(F32), axis=0, keepdims=True)
    cnt_ref[...] = cnt_acc[...]
    cols = [e1.astype(F32), e2.astype(F32), s1 / den, s2 / den, rank1, rank2]
    out = jnp.zeros((tm, LANES), F32)
    for c, col in enumerate(cols):
        out = jnp.where(lane == c, col, out)
    o_ref[...] = out


def _moe_dispatch_body(tm, dest_ref, x_ref, xs_in, xs_out, sem):
    del xs_in
    i = pl.program_id(0)

    def row_copy(r, j):
        d = dest_ref[2 * (i * tm + r) + j]
        return pltpu.make_async_copy(x_ref.at[pl.ds(r, 1), :], xs_out.at[pl.ds(d, 1), :], sem.at[j])

    def issue(r, carry):
        row_copy(r, 0).start()
        row_copy(r, 1).start()
        return carry

    lax.fori_loop(0, tm, issue, 0)

    def drain(r, carry):
        row_copy(r, 0).wait()
        row_copy(r, 1).wait()
        return carry

    lax.fori_loop(0, tm, drain, 0)


def _moe_expert_body(tile_e_ref, nt_ref, xs_ref, win_ref, wout_ref, o_ref):
    i = pl.program_id(0)

    @pl.when(i < nt_ref[0])
    def _():
        F = wout_ref.shape[2]
        h = _dot(xs_ref[...].astype(BF16), win_ref[0, 0])
        act = h[:, :F] * jax.nn.silu(h[:, F:])
        o_ref[...] = _dot(act.astype(BF16), wout_ref[0, 0])

    @pl.when(i >= nt_ref[0])
    def _():
        o_ref[...] = jnp.zeros_like(o_ref)


def _moe_combine_body(tm, alpha, pos_ref, y_hbm, x_ref, route_ref, lg_ref, lb_ref, o_ref, buf, sem):
    i = pl.program_id(0)

    def row_copy(r, j):
        p = pos_ref[2 * (i * tm + r) + j]
        return pltpu.make_async_copy(y_hbm.at[pl.ds(p, 1), :], buf.at[j, pl.ds(r, 1), :], sem.at[j])

    def issue(r, carry):
        row_copy(r, 0).start()
        row_copy(r, 1).start()
        return carry

    lax.fori_loop(0, tm, issue, 0)

    def drain(r, carry):
        row_copy(r, 0).wait()
        row_copy(r, 1).wait()
        return carry

    lax.fori_loop(0, tm, drain, 0)
    route = route_ref[...]
    z = alpha * x_ref[...] + (route[:, 2:3] * buf[0] + route[:, 3:4] * buf[1])
    o_ref[...] = _layer_norm(z, lg_ref[...], lb_ref[...])


def _moe_layer(x, router_w, router_b, w_in16, w_out16, layer, ln_g, ln_b, alpha, *, tm=256):
    T, D = x.shape
    E, F = w_out16.shape[1], w_out16.shape[2]
    tm = min(tm, T)
    rw_pad = jnp.pad(router_w, ((0, 0), (0, LANES - E)))
    rb_pad = jnp.pad(router_b, (0, LANES - E)).reshape(1, LANES)
    route, cnt = pl.pallas_call(
        _router_body,
        out_shape=[jax.ShapeDtypeStruct((T, LANES), F32), jax.ShapeDtypeStruct((1, LANES), F32)],
        grid=(T // tm,),
        in_specs=[pl.BlockSpec((tm, D), lambda i: (i, 0)), pl.BlockSpec((D, LANES), lambda i: (0, 0)),
                  pl.BlockSpec((1, LANES), lambda i: (0, 0))],
        out_specs=[pl.BlockSpec((tm, LANES), lambda i: (i, 0)), pl.BlockSpec((1, LANES), lambda i: (0, 0))],
        scratch_shapes=[pltpu.VMEM((1, LANES), F32)],
        compiler_params=_cparams(1), name="moe_router",
    )(x, rw_pad, rb_pad)
    counts = cnt[0, :E].astype(jnp.int32)
    padded = ((counts + tm - 1) // tm) * tm
    ends = jnp.cumsum(padded)
    starts = ends - padded
    sel = route[:, 0:2].astype(jnp.int32)
    dest = (starts[sel] + route[:, 4:6].astype(jnp.int32)).reshape(-1)
    n_rows = 2 * T + E * tm
    n_tiles = n_rows // tm
    used_tiles = (ends[-1] // tm).astype(jnp.int32).reshape(1)
    tile_start = jnp.minimum(jnp.arange(n_tiles, dtype=jnp.int32), used_tiles - 1) * tm
    tile_e = jnp.minimum(jnp.sum(ends[None, :] <= tile_start[:, None], axis=1), E - 1).astype(jnp.int32)

    x_sorted = pl.pallas_call(
        functools.partial(_moe_dispatch_body, tm),
        out_shape=jax.ShapeDtypeStruct((n_rows, D), F32),
        grid_spec=pltpu.PrefetchScalarGridSpec(
            num_scalar_prefetch=1, grid=(T // tm,),
            in_specs=[pl.BlockSpec((tm, D), lambda i, dst: (i, 0)), pl.BlockSpec(memory_space=pl.ANY)],
            out_specs=pl.BlockSpec(memory_space=pl.ANY),
            scratch_shapes=[pltpu.SemaphoreType.DMA((2,))],
        ),
        input_output_aliases={2: 0},
        compiler_params=_cparams(1), name="moe_dispatch",
    )(dest, x, jnp.zeros((n_rows, D), F32))

    live = lambda i, nt: jnp.minimum(i, nt[0] - 1)
    y_sorted = pl.pallas_call(
        _moe_expert_body,
        out_shape=jax.ShapeDtypeStruct((n_rows, D), F32),
        grid_spec=pltpu.PrefetchScalarGridSpec(
            num_scalar_prefetch=2, grid=(n_tiles,),
            in_specs=[
                pl.BlockSpec((tm, D), lambda i, te, nt: (live(i, nt), 0)),
                pl.BlockSpec((1, 1, D, 2 * F), lambda i, te, nt: (layer, te[i], 0, 0)),
                pl.BlockSpec((1, 1, F, D), lambda i, te, nt: (layer, te[i], 0, 0)),
            ],
            out_specs=pl.BlockSpec((tm, D), lambda i, te, nt: (i, 0)),
        ),
        compiler_params=_cparams(1), name="moe_experts",
    )(tile_e, used_tiles, x_sorted, w_in16, w_out16)

    row = lambda t: t.reshape(1, -1)
    return pl.pallas_call(
        functools.partial(_moe_combine_body, tm, alpha),
        out_shape=jax.ShapeDtypeStruct((T, D), F32),
        grid_spec=pltpu.PrefetchScalarGridSpec(
            num_scalar_prefetch=1, grid=(T // tm,),
            in_specs=[
                pl.BlockSpec(memory_space=pl.ANY),
                pl.BlockSpec((tm, D), lambda i, pos: (i, 0)),
                pl.BlockSpec((tm, LANES), lambda i, pos: (i, 0)),
                pl.BlockSpec((1, D), lambda i, pos: (0, 0)),
                pl.BlockSpec((1, D), lambda i, pos: (0, 0)),
            ],
            out_specs=pl.BlockSpec((tm, D), lambda i, pos: (i, 0)),
            scratch_shapes=[pltpu.VMEM((2, tm, D), F32), pltpu.SemaphoreType.DMA((2,))],
        ),
        compiler_params=_cparams(1), name="moe_combine",
    )(dest, y_sorted, x, route, row(ln_g), row(ln_b))


def kernel(x, rw_mu, rw_w_rkv, rw_w0, rw_w_a, rw_w_b, rw_a0, rw_a_a, rw_a_b, rw_v0, rw_v_a, rw_v_b, rw_g_a, rw_g_b, rw_k_k, rw_k_a, rw_r_k, rw_lnx_g, rw_lnx_b, rw_w_o, fx_w_kvf, fx_b_f, fx_k_norm, fx_w_qg, fx_q_norm, fx_w_o, router_w, router_b, moe_w_in, moe_w_out, ln_g, ln_b):
    B, T, D = x.shape
    depth = ln_g.shape[0]
    n_a = rw_mu.shape[0]
    alpha = (2 * depth) ** 0.25
    w_in16 = moe_w_in.astype(BF16)
    w_out16 = moe_w_out.astype(BF16)
    outs = []
    for bi in range(B):
        h = x[bi]
        v_first = None
        kv = None
        for l in range(depth):
            if l < n_a:
                vres = None if l == 0 else (rw_v0[l - 1], rw_v_a[l - 1], rw_v_b[l - 1])
                h, v_first = _rwkv_layer(
                    h, v_first, vres, rw_mu[l], rw_w_rkv[l], rw_w0[l], rw_w_a[l], rw_w_b[l], rw_a0[l],
                    rw_a_a[l], rw_a_b[l], rw_g_a[l], rw_g_b[l], rw_k_k[l], rw_k_a[l], rw_r_k[l],
                    rw_lnx_g[l], rw_lnx_b[l], rw_w_o[l], ln_g[l, 0], ln_b[l, 0], alpha)
            else:
                if kv is None:
                    kv = _fox_shared_kv(h, fx_w_kvf, fx_b_f, fx_k_norm)
                j = l - n_a
                h = _fox_layer(h, *kv, fx_w_qg[j], fx_q_norm[j], fx_w_o[j], ln_g[l, 0], ln_b[l, 0], alpha)
            h = _moe_layer(h, router_w, router_b, w_in16, w_out16, l, ln_g[l, 1], ln_b[l, 1], alpha)
        outs.append(h)
    return jnp.stack(outs, axis=0)
```

```python
import functools

import jax
import jax.numpy as jnp
from jax import lax
from jax.experimental import pallas as pl
from jax.experimental.pallas import tpu as pltpu

F32 = jnp.float32
BF16 = jnp.bfloat16

RWKV_HEAD = 64
FOX_HEAD = 128
N_EXPERTS = 16
N_GROUPS = 4
EXPERTS_PER_GROUP = N_EXPERTS // N_GROUPS
GN_EPS = 64e-5
LN_EPS = 1e-5
RMS_EPS = 1e-6
CHUNK = 64
PHASE1_PAIRS_PER_STEP = 16
LANES = 128
LOG2E = 1.4426950408889634
VMEM_LIMIT = 56 * 1024 * 1024


def _cparams(n_axes, vmem=VMEM_LIMIT):
    return pltpu.CompilerParams(dimension_semantics=("arbitrary",) * n_axes, vmem_limit_bytes=vmem)


def _dot(a, b):
    return jnp.dot(a, b, preferred_element_type=F32)


def _dot_nt(a, b):
    return lax.dot_general(a, b, (((1,), (1,)), ((), ())), preferred_element_type=F32)


def _split(a):
    hi = a.astype(BF16)
    lo = (a - hi.astype(F32)).astype(BF16)
    return hi, lo


def _dot3(a, b):
    ah, al = _split(a)
    bh, bl = _split(b)
    return _dot(ah, bh) + _dot(ah, bl) + _dot(al, bh)


def _layer_norm(z, g, b):
    mu = jnp.mean(z, axis=-1, keepdims=True)
    zc = z - mu
    var = jnp.mean(zc * zc, axis=-1, keepdims=True)
    return zc * lax.rsqrt(var + LN_EPS) * g + b


def _rowwise(body, row_args, const_args, outs, *, tm, name, scratch=()):
    T = row_args[0].shape[0]
    tm = min(tm, T)
    assert T % tm == 0
    in_specs = [pl.BlockSpec((tm, a.shape[1]), lambda i: (i, 0)) for a in row_args]
    in_specs += [pl.BlockSpec(a.shape, lambda i, nd=a.ndim: (0,) * nd) for a in const_args]
    out_shape = [jax.ShapeDtypeStruct((T, w), dt) for (w, dt) in outs]
    out_specs = [pl.BlockSpec((tm, w), lambda i: (i, 0)) for (w, dt) in outs]
    res = pl.pallas_call(
        body, out_shape=out_shape, grid=(T // tm,), in_specs=in_specs, out_specs=out_specs,
        scratch_shapes=list(scratch), compiler_params=_cparams(1), name=name,
    )(*row_args, *const_args)
    return res


def _mix_proj_body(x_ref, xp_ref, mu_ref, w_ref, o_ref):
    x = x_ref[...]
    xs = x + (xp_ref[...] - x) * mu_ref[...]
    o_ref[...] = _dot(xs.astype(BF16), w_ref[...])


def _lora_body(has_v, x_ref, xp_ref, mu_ref, w0_ref, wa_ref, wb_ref, a0_ref, aa_ref, ab_ref,
               ga_ref, gb_ref, *rest):
    if has_v:
        v0_ref, va_ref, vb_ref, logw_ref, a_ref, g_ref, vg_ref = rest
    else:
        logw_ref, a_ref, g_ref = rest
    x = x_ref[...]
    dx = xp_ref[...] - x
    mix = lambda c: (x + dx * mu_ref[c:c + 1, :]).astype(BF16)
    w_raw = w0_ref[...] + _dot(jnp.tanh(_dot(mix(3), wa_ref[...])).astype(BF16), wb_ref[...])
    logw_ref[...] = (-0.6065306597126334) * jax.nn.sigmoid(w_raw)
    a_ref[...] = jax.nn.sigmoid(a0_ref[...] + _dot(_dot(mix(4), aa_ref[...]).astype(BF16), ab_ref[...]))
    g_ref[...] = _dot(jax.nn.sigmoid(_dot(mix(5), ga_ref[...])).astype(BF16), gb_ref[...])
    if has_v:
        vg_ref[...] = jax.nn.sigmoid(v0_ref[...] + _dot(_dot(mix(2), va_ref[...]).astype(BF16), vb_ref[...]))


def _out_proj_ln_body(alpha, y_ref, g_ref, x_ref, w_ref, lg_ref, lb_ref, o_ref):
    h = _dot((y_ref[...].astype(F32) * g_ref[...].astype(F32)).astype(BF16), w_ref[...])
    o_ref[...] = _layer_norm(alpha * x_ref[...] + h, lg_ref[...], lb_ref[...])


def _proj_rms_aug_body(scale, is_query, x_ref, c_ref, w_ref, gain_ref, o_ref):
    acc = _dot(x_ref[...].astype(BF16), w_ref[...])
    c2 = c_ref[...] * LOG2E
    lane = lax.broadcasted_iota(jnp.int32, c2.shape, 1)
    for h in range(acc.shape[1] // FOX_HEAD):
        t = acc[:, h * FOX_HEAD:(h + 1) * FOX_HEAD]
        ms = jnp.mean(t * t, axis=-1, keepdims=True)
        base = 2 * h * FOX_HEAD
        o_ref[:, base:base + FOX_HEAD] = (t * lax.rsqrt(ms + RMS_EPS) * gain_ref[...] * scale).astype(o_ref.dtype)
        ch = jnp.sum(jnp.where(lane == h, c2, 0.0), axis=-1, keepdims=True)
        if not is_query:
            ch = -ch
        hi = ch.astype(BF16).astype(F32)
        mid = (ch - hi).astype(BF16).astype(F32)
        low = ch - hi - mid
        t0 = 0 if is_query else 3
        one = ((lane >= 3 - t0) & (lane < 6 - t0)).astype(F32)
        aug = jnp.where(lane == t0, hi, jnp.where(lane == t0 + 1, mid, jnp.where(lane == t0 + 2, low, one)))
        o_ref[:, base + FOX_HEAD:base + 2 * FOX_HEAD] = aug.astype(o_ref.dtype)


def _proj_body(x_ref, w_ref, o_ref):
    o_ref[...] = _dot(x_ref[...].astype(BF16), w_ref[...]).astype(o_ref.dtype)


def _proj_sigmoid_body(x_ref, w_ref, o_ref):
    o_ref[...] = jax.nn.sigmoid(_dot(x_ref[...].astype(BF16), w_ref[...])).astype(o_ref.dtype)


def _log_sigmoid(z):
    return jnp.minimum(z, 0.0) - jnp.log(1.0 + jnp.exp(-jnp.abs(z)))


def _forget_cumsum_body(x_ref, w_ref, b_ref, c_ref, carry_ref):
    @pl.when(pl.program_id(0) == 0)
    def _():
        carry_ref[...] = jnp.zeros_like(carry_ref)

    logf = _log_sigmoid(_dot(x_ref[...].astype(BF16), w_ref[...]) + b_ref[...])
    tm = logf.shape[0]
    tri = (lax.broadcasted_iota(jnp.int32, (tm, tm), 0) >= lax.broadcasted_iota(jnp.int32, (tm, tm), 1))
    tri = tri.astype(BF16)
    hi = logf.astype(BF16)
    r1 = logf - hi.astype(F32)
    mid = r1.astype(BF16)
    lo = (r1 - mid.astype(F32)).astype(BF16)
    c = _dot(tri, hi) + _dot(tri, mid) + _dot(tri, lo) + carry_ref[...]
    c_ref[...] = c
    carry_ref[...] = c[tm - 1:tm, :]


def _half_sum(x, lo):
    s_lo = jnp.sum(jnp.where(lo, x, 0.0), axis=-1, keepdims=True)
    s_hi = jnp.sum(jnp.where(lo, 0.0, x), axis=-1, keepdims=True)
    return jnp.where(lo, s_lo, s_hi)


def _to_dd(x, lo, roll):
    xr = roll(x, RWKV_HEAD, 1)
    return jnp.concatenate([jnp.where(lo, x, xr), jnp.where(lo, xr, x)], axis=0)


def _interleave(gens):
    gens = list(gens)
    while gens:
        alive = []
        for gen in gens:
            try:
                next(gen)
                alive.append(gen)
            except StopIteration:
                pass
        gens = alive


def _chunk_phase1(r, logw, k_raw, v, a, k_k, k_a, r_k, roll, emit):
    C = r.shape[0]
    lane = lax.broadcasted_iota(jnp.int32, (C, LANES), 1)
    lo = lane < RWKV_HEAD
    kk = k_raw * k_k
    kk = kk / jnp.maximum(jnp.sqrt(_half_sum(kk * kk, lo)), 1e-12)
    k = k_raw * (1.0 + (a - 1.0) * k_a)
    bonus = _half_sum(r * k * r_k, lo) * v
    b = kk * a
    row = lax.broadcasted_iota(jnp.int32, (C, LANES), 0)
    cw = logw
    s = 1
    while s < C:
        cw = cw + jnp.where(row >= s, roll(cw, s, 0), 0.0)
        s *= 2
    cw_end = cw[C - 1:C, :]
    g_in = jnp.exp(cw)
    g_ex = jnp.exp(cw - logw)
    g_inv = jnp.exp(-cw)
    g_rem = jnp.exp(cw_end - cw)
    dd = lambda x: _to_dd(x, lo, roll)
    a_t = dd(-kk * g_ex)
    r_t = dd(r * g_in)
    k_h = dd(k * g_inv)
    b_h = dd(b * g_inv)
    k_g = dd(k * g_rem)
    b_g = dd(b * g_rem)
    v_d = dd(v)
    ri = lax.broadcasted_iota(jnp.int32, (2 * C, 2 * C), 0)
    ci = lax.broadcasted_iota(jnp.int32, (2 * C, 2 * C), 1)
    same = (ri < C) == (ci < C)
    strict = same & (ri > ci)
    incl = same & (ri >= ci)
    ar = jnp.concatenate([a_t, r_t], axis=0).astype(BF16)
    kb = jnp.concatenate([k_h, b_h], axis=0).astype(BF16)
    yield
    prod = 0.5 * _dot_nt(ar, kb)
    yield
    a_ak = jnp.where(strict, prod[:2 * C, :2 * C], 0.0)
    a_ab = jnp.where(strict, prod[:2 * C, 2 * C:], 0.0)
    a_rk = jnp.where(incl, prod[2 * C:, :2 * C], 0.0)
    a_rb = jnp.where(incl, prod[2 * C:, 2 * C:], 0.0)
    v16 = v_d.astype(BF16)
    n16 = a_ab.astype(BF16)
    npow = _dot(n16, n16)
    akv = _dot(jnp.concatenate([a_ak, a_rk], axis=0).astype(BF16), v16)
    yield
    tinv = (ri == ci).astype(F32) + a_ab
    s = 4
    while s < C:
        both = _dot(npow.astype(BF16), jnp.concatenate([tinv, npow], axis=1).astype(BF16))
        yield
        tinv = tinv + both[:, :2 * C]
        npow = both[:, 2 * C:]
        s *= 2
    tinv = tinv + _dot(npow.astype(BF16), tinv.astype(BF16))
    yield
    pq = _dot(tinv.astype(BF16), jnp.concatenate([a_t, akv[:2 * C]], axis=1).astype(BF16))
    yield
    pq16 = pq.astype(BF16)
    rbpq = _dot(a_rb.astype(BF16), pq16)
    kbg_t = jnp.concatenate([jnp.where(same, k_g, 0.0), jnp.where(same, b_g, 0.0)], axis=0).T
    rhs = jnp.concatenate([jnp.concatenate([v16, jnp.zeros_like(v16)], axis=1),
                           jnp.concatenate([pq16[:, LANES:], pq16[:, :LANES]], axis=1)], axis=0)
    zm = _dot(kbg_t.astype(BF16), rhs)
    yield
    g = r_t + rbpq[:, :LANES]
    yi = akv[2 * C:] + rbpq[:, LANES:]
    m_bd = jnp.where(same, zm[:, LANES:], 0.0) + jnp.where(ri == ci, jnp.exp(cw_end), 0.0)
    emit(jnp.where(same, g, 0.0), m_bd, zm[:, :LANES], jnp.where(lo, yi[:C], yi[C:]), bonus)


def _rwkv_phase1_body(npairs, has_v, r_ref, k_ref, v_ref, lw_ref, a_ref, *rest):
    if has_v:
        vf_ref, vg_ref, kk_ref, ka_ref, rk_ref, g_ref, m_ref, z_ref, yi_ref, bo_ref = rest
    else:
        kk_ref, ka_ref, rk_ref, g_ref, m_ref, z_ref, yi_ref, bo_ref = rest
    roll = lambda x, s, ax: pltpu.roll(x, s, ax)

    def one_pair(p):
        sl = slice(p * LANES, (p + 1) * LANES)
        v = v_ref[:, sl]
        if has_v:
            v = v + (vf_ref[:, sl] - v) * vg_ref[:, sl]

        def emit(g_bd, m_bd, z, yi, bonus):
            g_ref[0, :, sl] = g_bd
            m_ref[0, :, sl] = m_bd
            z_ref[0, :, sl] = z
            yi_ref[:, sl] = yi
            bo_ref[:, sl] = bonus

        return _chunk_phase1(r_ref[:, sl], lw_ref[:, sl], k_ref[:, sl], v, a_ref[:, sl],
                             kk_ref[:, sl], ka_ref[:, sl], rk_ref[:, sl], roll, emit)

    _interleave(one_pair(p) for p in range(npairs))


def _rwkv_phase1(r, k, v, logw, a, v_first, vgate, k_k, k_a, r_k, *, pairs_per_step=PHASE1_PAIRS_PER_STEP):
    T, D = r.shape
    C = CHUNK
    n_chunks = T // C
    n_pairs = D // LANES
    pp = min(pairs_per_step, n_pairs)
    assert n_pairs % pp == 0 and T % C == 0
    has_v = v_first is not None
    W = pp * LANES
    row_spec = pl.BlockSpec((C, W), lambda c, j: (c, j))
    par_spec = pl.BlockSpec((1, W), lambda c, j: (0, j))
    mat_spec = pl.BlockSpec((1, 2 * C, W), lambda c, j: (c, 0, j))
    ins = [r, k, v, logw, a] + ([v_first, vgate] if has_v else []) + [k_k, k_a, r_k]
    in_specs = [row_spec] * (7 if has_v else 5) + [par_spec] * 3
    mat_shape = jax.ShapeDtypeStruct((n_chunks, 2 * C, D), F32)
    row_shape = jax.ShapeDtypeStruct((T, D), F32)
    out_shape = [mat_shape, mat_shape, mat_shape, row_shape, row_shape]
    out_specs = [mat_spec, mat_spec, mat_spec, row_spec, row_spec]
    return pl.pallas_call(
        functools.partial(_rwkv_phase1_body, pp, has_v),
        out_shape=out_shape, grid=(n_chunks, n_pairs // pp), in_specs=in_specs, out_specs=out_specs,
        compiler_params=_cparams(2), name="rwkv_phase1",
    )(*ins)


def _rwkv_phase2_body(npairs, g_ref, m_ref, z_ref, yi_ref, bo_ref, lg_ref, lb_ref, y_ref, s_ref):
    @pl.when(pl.program_id(0) == 0)
    def _():
        s_ref[...] = jnp.zeros_like(s_ref)

    C = CHUNK
    lane = lax.broadcasted_iota(jnp.int32, (C, LANES), 1)
    lo = lane < RWKV_HEAD

    def one_pair(p):
        sl = slice(p * LANES, (p + 1) * LANES)
        s0 = s_ref[p]
        gm = jnp.concatenate([g_ref[0, :, sl], m_ref[0, :, sl]], axis=0)
        g_hi, g_lo = _split(gm)
        s_hi, s_lo = _split(s0)
        yield
        res = _dot(g_hi, s_hi) + _dot(g_hi, s_lo) + _dot(g_lo, s_hi)
        yield
        s_ref[p] = res[2 * C:] + z_ref[0, :, sl]
        yd = res[:2 * C]
        y = jnp.where(lo, yd[:C], yd[C:]) + yi_ref[:, sl]
        mean = _half_sum(y, lo) * (1.0 / RWKV_HEAD)
        yc = y - mean
        var = _half_sum(yc * yc, lo) * (1.0 / RWKV_HEAD)
        y_ref[:, sl] = yc * lax.rsqrt(var + GN_EPS) * lg_ref[:, sl] + lb_ref[:, sl] + bo_ref[:, sl]

    _interleave(one_pair(p) for p in range(npairs))


def _rwkv_phase2(g_bd, m_bd, z, yi, bonus, lnx_g, lnx_b):
    n_chunks, C2, D = g_bd.shape
    C = CHUNK
    T = n_chunks * C
    n_pairs = D // LANES
    row_spec = pl.BlockSpec((C, D), lambda c: (c, 0))
    par_spec = pl.BlockSpec((1, D), lambda c: (0, 0))
    mat_spec = pl.BlockSpec((1, 2 * C, D), lambda c: (c, 0, 0))
    return pl.pallas_call(
        functools.partial(_rwkv_phase2_body, n_pairs),
        out_shape=jax.ShapeDtypeStruct((T, D), F32),
        grid=(n_chunks,),
        in_specs=[mat_spec, mat_spec, mat_spec, row_spec, row_spec, par_spec, par_spec],
        out_specs=row_spec,
        scratch_shapes=[pltpu.VMEM((n_pairs, 2 * C, LANES), F32)],
        compiler_params=_cparams(1), name="rwkv_phase2",
    )(g_bd, m_bd, z, yi, bonus, lnx_g, lnx_b)


def _rwkv_layer(x, v_first, vres, mu, w_rkv, w0, w_a, w_b, a0, a_a, a_b, g_a, g_b, k_k, k_a, r_k,
                lnx_g, lnx_b, w_o, ln_g, ln_b, alpha):
    T, D = x.shape
    xp = jnp.concatenate([jnp.zeros((1, D), x.dtype), x[:-1]], axis=0)
    row = lambda t: t.reshape(1, -1)
    padc = lambda w: jnp.pad(w, ((0, 0), (0, (-w.shape[1]) % LANES))).astype(BF16)
    padr = lambda w: jnp.pad(w, ((0, (-w.shape[0]) % LANES), (0, 0))).astype(BF16)
    rkv = [
        _rowwise(_mix_proj_body, [x, xp], [row(mu[c]), w_rkv[c].astype(BF16)], [(D, F32)],
                 tm=256, name=f"rwkv_proj{c}")[0]
        for c in range(3)
    ]
    consts = [mu, row(w0), padc(w_a), padr(w_b), row(a0), padc(a_a), padr(a_b), padc(g_a), padr(g_b)]
    outs = [(D, F32)] * 3
    has_v = vres is not None
    if has_v:
        v0, v_a, v_b = vres
        consts += [row(v0), padc(v_a), padr(v_b)]
        outs = outs + [(D, F32)]
    lora = _rowwise(functools.partial(_lora_body, has_v), [x, xp], consts, outs, tm=256, name="rwkv_lora")
    logw, a, g = lora[:3]
    vgate = lora[3] if has_v else None
    ph1 = _rwkv_phase1(rkv[0], rkv[1], rkv[2], logw, a, v_first if has_v else None, vgate,
                       row(k_k), row(k_a), r_k.reshape(1, D))
    g_bd, m_bd, z, yi, bonus = ph1[:5]
    if not has_v:
        v_first = rkv[2]
    y = _rwkv_phase2(g_bd, m_bd, z, yi, bonus, row(lnx_g), row(lnx_b))
    x_new = _rowwise(functools.partial(_out_proj_ln_body, alpha), [y, g, x],
                     [w_o.astype(BF16), row(ln_g), row(ln_b)], [(D, F32)], tm=256, name="rwkv_out")[0]
    return x_new, v_first


def _fox_flash_body(tq, q_ref, k_ref, v_ref, o_ref, m_ref, l_ref, acc_ref):
    i = pl.program_id(1)
    m_ref[...] = jnp.full_like(m_ref, -1e30)
    l_ref[...] = jnp.zeros_like(l_ref)
    acc_ref[...] = jnp.zeros_like(acc_ref)

    def block(j, diagonal):
        start = pl.multiple_of(j * tq, tq)
        s = _dot(q_ref[...], k_ref[0, j])
        if diagonal:
            row = lax.broadcasted_iota(jnp.int32, (tq, tq), 0)
            col = lax.broadcasted_iota(jnp.int32, (tq, tq), 1)
            s = jnp.where(col <= row, s, -1e30)
        chunks = [s[:, c * LANES:(c + 1) * LANES] for c in range(tq // LANES)]
        m_loc = functools.reduce(jnp.maximum, chunks)
        m_old = m_ref[...]
        m_new = jnp.maximum(m_old, jnp.max(m_loc, axis=-1, keepdims=True))
        alpha = jnp.exp2(m_old - m_new)
        ps = [jnp.exp2(c - m_new) for c in chunks]
        l_ref[...] = alpha * l_ref[...] + functools.reduce(jnp.add, ps)
        p = jnp.concatenate(ps, axis=1).astype(BF16)
        acc_ref[...] = alpha * acc_ref[...] + _dot(p, v_ref[pl.ds(start, tq), :])
        m_ref[...] = m_new

    def pair(jj, carry):
        block(2 * jj, False)
        block(2 * jj + 1, False)
        return carry

    lax.fori_loop(0, i // 2, pair, 0)

    @pl.when(i % 2 == 1)
    def _():
        block(i - 1, False)

    block(i, True)
    o_ref[...] = (acc_ref[...] / jnp.sum(l_ref[...], axis=-1, keepdims=True)).astype(o_ref.dtype)


def _fox_flash(q_aug, k_aug, v, *, tq=512):
    T, D = v.shape
    H = D // FOX_HEAD
    tq = min(tq, T)
    k_t = k_aug.reshape(T // tq, tq, H, 2 * FOX_HEAD).transpose(2, 0, 3, 1)
    return pl.pallas_call(
        functools.partial(_fox_flash_body, tq),
        out_shape=jax.ShapeDtypeStruct((T, D), F32),
        grid=(H, T // tq),
        in_specs=[
            pl.BlockSpec((tq, 2 * FOX_HEAD), lambda h, i: (i, h)),
            pl.BlockSpec((1, T // tq, 2 * FOX_HEAD, tq), lambda h, i: (h, 0, 0, 0)),
            pl.BlockSpec((T, FOX_HEAD), lambda h, i: (0, h)),
        ],
        out_specs=pl.BlockSpec((tq, FOX_HEAD), lambda h, i: (i, h)),
        scratch_shapes=[pltpu.VMEM((tq, LANES), F32), pltpu.VMEM((tq, LANES), F32),
                        pltpu.VMEM((tq, FOX_HEAD), F32)],
        compiler_params=_cparams(2), name="fox_flash",
    )(q_aug, k_t, v)


def _fox_shared_kv(x, w_kvf, b_f, k_norm):
    T, D = x.shape
    H = D // FOX_HEAD
    w_f = jnp.pad(w_kvf[:, 2 * D:], ((0, 0), (0, LANES - H))).astype(BF16)
    b_pad = jnp.pad(b_f, (0, LANES - H)).reshape(1, LANES)
    c = _rowwise(_forget_cumsum_body, [x], [w_f, b_pad], [(LANES, F32)], tm=256, name="fox_forget",
                 scratch=[pltpu.VMEM((1, LANES), F32)])[0]
    k_aug = _rowwise(functools.partial(_proj_rms_aug_body, 1.0, False), [x, c],
                     [w_kvf[:, :D].astype(BF16), k_norm.reshape(1, FOX_HEAD)], [(2 * D, BF16)], tm=256,
                     name="fox_k")[0]
    v = _rowwise(_proj_body, [x], [w_kvf[:, D:2 * D].astype(BF16)], [(D, BF16)], tm=256, name="fox_v")[0]
    return k_aug, v, c


def _fox_layer(x, k_aug, v, c, w_qg, q_norm, w_o, ln_g, ln_b, alpha):
    T, D = x.shape
    row = lambda t: t.reshape(1, -1)
    q_aug = _rowwise(functools.partial(_proj_rms_aug_body, FOX_HEAD ** -0.5 * LOG2E, True), [x, c],
                     [w_qg[:, :D].astype(BF16), q_norm.reshape(1, FOX_HEAD)], [(2 * D, BF16)], tm=256,
                     name="fox_q")[0]
    gate = _rowwise(_proj_sigmoid_body, [x], [w_qg[:, D:].astype(BF16)], [(D, F32)], tm=256, name="fox_gate")[0]
    o = _fox_flash(q_aug, k_aug, v)
    return _rowwise(functools.partial(_out_proj_ln_body, alpha), [o, gate, x],
                    [w_o.astype(BF16), row(ln_g), row(ln_b)], [(D, F32)], tm=256, name="fox_out")[0]


def _router_body(x_ref, w_ref, b_ref, o_ref, cnt_ref, cnt_acc):
    @pl.when(pl.program_id(0) == 0)
    def _():
        cnt_acc[...] = jnp.zeros_like(cnt_acc)

    logits = _dot3(x_ref[...], w_ref[...])
    scores = jax.nn.sigmoid(logits)
    biased = scores + b_ref[...]
    sc = [scores[:, e:e + 1] for e in range(N_EXPERTS)]
    bi = [biased[:, e:e + 1] for e in range(N_EXPERTS)]
    gs = []
    for g in range(N_GROUPS):
        a, b, c, d = bi[4 * g:4 * g + 4]
        hi1, lo1 = jnp.maximum(a, b), jnp.minimum(a, b)
        hi2, lo2 = jnp.maximum(c, d), jnp.minimum(c, d)
        gs.append(jnp.maximum(hi1, hi2) + jnp.maximum(jnp.minimum(hi1, hi2), jnp.maximum(lo1, lo2)))
    best, g_sel = gs[0], jnp.zeros_like(gs[0], dtype=jnp.int32)
    for g in range(1, N_GROUPS):
        better = gs[g] > best
        best = jnp.where(better, gs[g], best)
        g_sel = jnp.where(better, g, g_sel)

    def pick(cols, j):
        out = cols[j]
        for g in range(1, N_GROUPS):
            out = jnp.where(g_sel == g, cols[4 * g + j], out)
        return out

    ib = [pick(bi, j) for j in range(EXPERTS_PER_GROUP)]
    isc = [pick(sc, j) for j in range(EXPERTS_PER_GROUP)]

    def argmax4(vals):
        bv, bi_, bs = vals[0], jnp.zeros_like(g_sel), isc[0]
        for j in range(1, EXPERTS_PER_GROUP):
            better = vals[j] > bv
            bv = jnp.where(better, vals[j], bv)
            bi_ = jnp.where(better, j, bi_)
            bs = jnp.where(better, isc[j], bs)
        return bi_, bs

    i1, s1 = argmax4(ib)
    ib2 = [jnp.where(i1 == j, -jnp.inf, ib[j]) for j in range(EXPERTS_PER_GROUP)]
    i2, s2 = argmax4(ib2)
    den = s1 + s2
    e1 = g_sel * EXPERTS_PER_GROUP + i1
    e2 = g_sel * EXPERTS_PER_GROUP + i2
    tm = logits.shape[0]
    lane = lax.broadcasted_iota(jnp.int32, (tm, LANES), 1)
    oh1 = lane == e1
    oh2 = lane == e2
    oh = oh1 | oh2
    before = lax.broadcasted_iota(jnp.int32, (tm, tm), 0) > lax.broadcasted_iota(jnp.int32, (tm, tm), 1)
    prefix = _dot(before.astype(BF16), oh.astype(BF16)) + cnt_acc[...]
    rank1 = jnp.sum(jnp.where(oh1, prefix, 0.0), axis=-1, keepdims=True)
    rank2 = jnp.sum(jnp.where(oh2, prefix, 0.0), axis=-1, keepdims=True)
    cnt_acc[...] = cnt_acc[...] + jnp.sum(oh.astype(F32), axis=0, keepdims=True)
    cnt_ref[...] = cnt_acc[...]
    cols = [e1.astype(F32), e2.astype(F32), s1 / den, s2 / den, rank1, rank2]
    out = jnp.zeros((tm, LANES), F32)
    for c, col in enumerate(cols):
        out = jnp.where(lane == c, col, out)
    o_ref[...] = out


def _moe_dispatch_body(tm, dest_ref, x_ref, xs_in, xs_out, sem):
    del xs_in
    i = pl.program_id(0)

    def row_copy(r, j):
        d = dest_ref[2 * (i * tm + r) + j]
        return pltpu.make_async_copy(x_ref.at[pl.ds(r, 1), :], xs_out.at[pl.ds(d, 1), :], sem.at[j])

    def issue(r, carry):
        row_copy(r, 0).start()
        row_copy(r, 1).start()
        return carry

    lax.fori_loop(0, tm, issue, 0)

    def drain(r, carry):
        row_copy(r, 0).wait()
        row_copy(r, 1).wait()
        return carry

    lax.fori_loop(0, tm, drain, 0)


def _moe_expert_body(tile_e_ref, nt_ref, xs_ref, win_ref, wout_ref, o_ref):
    i = pl.program_id(0)

    @pl.when(i < nt_ref[0])
    def _():
        F = wout_ref.shape[2]
        h = _dot(xs_ref[...].astype(BF16), win_ref[0, 0])
        act = h[:, :F] * jax.nn.silu(h[:, F:])
        o_ref[...] = _dot(act.astype(BF16), wout_ref[0, 0])

    @pl.when(i >= nt_ref[0])
    def _():
        o_ref[...] = jnp.zeros_like(o_ref)


def _moe_combine_body(tm, alpha, pos_ref, y_hbm, x_ref, route_ref, lg_ref, lb_ref, o_ref, buf, sem):
    i = pl.program_id(0)

    def row_copy(r, j):
        p = pos_ref[2 * (i * tm + r) + j]
        return pltpu.make_async_copy(y_hbm.at[pl.ds(p, 1), :], buf.at[j, pl.ds(r, 1), :], sem.at[j])

    def issue(r, carry):
        row_copy(r, 0).start()
        row_copy(r, 1).start()
        return carry

    lax.fori_loop(0, tm, issue, 0)

    def drain(r, carry):
        row_copy(r, 0).wait()
        row_copy(r, 1).wait()
        return carry

    lax.fori_loop(0, tm, drain, 0)
    route = route_ref[...]
    z = alpha * x_ref[...] + (route[:, 2:3] * buf[0] + route[:, 3:4] * buf[1])
    o_ref[...] = _layer_norm(z, lg_ref[...], lb_ref[...])


def _moe_layer(x, router_w, router_b, w_in16, w_out16, layer, ln_g, ln_b, alpha, xs_init=None, *, tm=256):
    T, D = x.shape
    E, F = w_out16.shape[1], w_out16.shape[2]
    tm = min(tm, T)
    rw_pad = jnp.pad(router_w, ((0, 0), (0, LANES - E)))
    rb_pad = jnp.pad(router_b, (0, LANES - E)).reshape(1, LANES)
    route, cnt = pl.pallas_call(
        _router_body,
        out_shape=[jax.ShapeDtypeStruct((T, LANES), F32), jax.ShapeDtypeStruct((1, LANES), F32)],
        grid=(T // tm,),
        in_specs=[pl.BlockSpec((tm, D), lambda i: (i, 0)), pl.BlockSpec((D, LANES), lambda i: (0, 0)),
                  pl.BlockSpec((1, LANES), lambda i: (0, 0))],
        out_specs=[pl.BlockSpec((tm, LANES), lambda i: (i, 0)), pl.BlockSpec((1, LANES), lambda i: (0, 0))],
        scratch_shapes=[pltpu.VMEM((1, LANES), F32)],
        compiler_params=_cparams(1), name="moe_router",
    )(x, rw_pad, rb_pad)
    counts = cnt[0, :E].astype(jnp.int32)
    padded = ((counts + tm - 1) // tm) * tm
    ends = jnp.cumsum(padded)
    starts = ends - padded
    sel = route[:, 0:2].astype(jnp.int32)
    dest = (starts[sel] + route[:, 4:6].astype(jnp.int32)).reshape(-1)
    n_rows = 2 * T + E * tm
    n_tiles = n_rows // tm
    used_tiles = (ends[-1] // tm).astype(jnp.int32).reshape(1)
    tile_start = jnp.minimum(jnp.arange(n_tiles, dtype=jnp.int32), used_tiles - 1) * tm
    tile_e = jnp.minimum(jnp.sum(ends[None, :] <= tile_start[:, None], axis=1), E - 1).astype(jnp.int32)

    x_sorted = pl.pallas_call(
        functools.partial(_moe_dispatch_body, tm),
        out_shape=jax.ShapeDtypeStruct((n_rows, D), F32),
        grid_spec=pltpu.PrefetchScalarGridSpec(
            num_scalar_prefetch=1, grid=(T // tm,),
            in_specs=[pl.BlockSpec((tm, D), lambda i, dst: (i, 0)), pl.BlockSpec(memory_space=pl.ANY)],
            out_specs=pl.BlockSpec(memory_space=pl.ANY),
            scratch_shapes=[pltpu.SemaphoreType.DMA((2,))],
        ),
        input_output_aliases={2: 0},
        compiler_params=_cparams(1), name="moe_dispatch",
    )(dest, x, jnp.zeros((n_rows, D), F32) if xs_init is None else xs_init)

    live = lambda i, nt: jnp.minimum(i, nt[0] - 1)
    y_sorted = pl.pallas_call(
        _moe_expert_body,
        out_shape=jax.ShapeDtypeStruct((n_rows, D), F32),
        grid_spec=pltpu.PrefetchScalarGridSpec(
            num_scalar_prefetch=2, grid=(n_tiles,),
            in_specs=[
                pl.BlockSpec((tm, D), lambda i, te, nt: (live(i, nt), 0)),
                pl.BlockSpec((1, 1, D, 2 * F), lambda i, te, nt: (layer, te[i], 0, 0)),
                pl.BlockSpec((1, 1, F, D), lambda i, te, nt: (layer, te[i], 0, 0)),
            ],
            out_specs=pl.BlockSpec((tm, D), lambda i, te, nt: (i, 0)),
        ),
        compiler_params=_cparams(1), name="moe_experts",
    )(tile_e, used_tiles, x_sorted, w_in16, w_out16)

    row = lambda t: t.reshape(1, -1)
    out = pl.pallas_call(
        functools.partial(_moe_combine_body, tm, alpha),
        out_shape=jax.ShapeDtypeStruct((T, D), F32),
        grid_spec=pltpu.PrefetchScalarGridSpec(
            num_scalar_prefetch=1, grid=(T // tm,),
            in_specs=[
                pl.BlockSpec(memory_space=pl.ANY),
                pl.BlockSpec((tm, D), lambda i, pos: (i, 0)),
                pl.BlockSpec((tm, LANES), lambda i, pos: (i, 0)),
                pl.BlockSpec((1, D), lambda i, pos: (0, 0)),
                pl.BlockSpec((1, D), lambda i, pos: (0, 0)),
            ],
            out_specs=pl.BlockSpec((tm, D), lambda i, pos: (i, 0)),
            scratch_shapes=[pltpu.VMEM((2, tm, D), F32), pltpu.SemaphoreType.DMA((2,))],
        ),
        compiler_params=_cparams(1), name="moe_combine",
    )(dest, y_sorted, x, route, row(ln_g), row(ln_b))
    return out, x_sorted


def kernel(x, rw_mu, rw_w_rkv, rw_w0, rw_w_a, rw_w_b, rw_a0, rw_a_a, rw_a_b, rw_v0, rw_v_a, rw_v_b, rw_g_a, rw_g_b, rw_k_k, rw_k_a, rw_r_k, rw_lnx_g, rw_lnx_b, rw_w_o, fx_w_kvf, fx_b_f, fx_k_norm, fx_w_qg, fx_q_norm, fx_w_o, router_w, router_b, moe_w_in, moe_w_out, ln_g, ln_b):
    B, T, D = x.shape
    depth = ln_g.shape[0]
    n_a = rw_mu.shape[0]
    alpha = (2 * depth) ** 0.25
    w_in16 = moe_w_in.astype(BF16)
    w_out16 = moe_w_out.astype(BF16)
    outs = []
    for bi in range(B):
        h = x[bi]
        v_first = None
        kv = None
        xs_buf = None
        for l in range(depth):
            if l < n_a:
                vres = None if l == 0 else (rw_v0[l - 1], rw_v_a[l - 1], rw_v_b[l - 1])
                h, v_first = _rwkv_layer(
                    h, v_first, vres, rw_mu[l], rw_w_rkv[l], rw_w0[l], rw_w_a[l], rw_w_b[l], rw_a0[l],
                    rw_a_a[l], rw_a_b[l], rw_g_a[l], rw_g_b[l], rw_k_k[l], rw_k_a[l], rw_r_k[l],
                    rw_lnx_g[l], rw_lnx_b[l], rw_w_o[l], ln_g[l, 0], ln_b[l, 0], alpha)
            else:
                if kv is None:
                    kv = _fox_shared_kv(h, fx_w_kvf, fx_b_f, fx_k_norm)
                j = l - n_a
                h = _fox_layer(h, *kv, fx_w_qg[j], fx_q_norm[j], fx_w_o[j], ln_g[l, 0], ln_b[l, 0], alpha)
            h, xs_buf = _moe_layer(h, router_w, router_b, w_in16, w_out16, l, ln_g[l, 1], ln_b[l, 1], alpha, xs_buf)
        outs.append(h)
    return jnp.stack(outs, axis=0)
```

```python
import functools

import jax
import jax.numpy as jnp
from jax import lax
from jax.experimental import pallas as pl
from jax.experimental.pallas import tpu as pltpu

F32 = jnp.float32
BF16 = jnp.bfloat16

RWKV_HEAD = 64
FOX_HEAD = 128
N_EXPERTS = 16
N_GROUPS = 4
EXPERTS_PER_GROUP = N_EXPERTS // N_GROUPS
GN_EPS = 64e-5
LN_EPS = 1e-5
RMS_EPS = 1e-6
CHUNK = 64
PHASE1_PAIRS_PER_STEP = 16
LANES = 128
LOG2E = 1.4426950408889634
VMEM_LIMIT = 56 * 1024 * 1024


def _cparams(n_axes, vmem=VMEM_LIMIT):
    return pltpu.CompilerParams(dimension_semantics=("arbitrary",) * n_axes, vmem_limit_bytes=vmem)


def _dot(a, b):
    return jnp.dot(a, b, preferred_element_type=F32)


def _dot_nt(a, b):
    return lax.dot_general(a, b, (((1,), (1,)), ((), ())), preferred_element_type=F32)


def _split(a):
    hi = a.astype(BF16)
    lo = (a - hi.astype(F32)).astype(BF16)
    return hi, lo


def _dot3(a, b):
    ah, al = _split(a)
    bh, bl = _split(b)
    return _dot(ah, bh) + _dot(ah, bl) + _dot(al, bh)


def _layer_norm(z, g, b):
    mu = jnp.mean(z, axis=-1, keepdims=True)
    zc = z - mu
    var = jnp.mean(zc * zc, axis=-1, keepdims=True)
    return zc * lax.rsqrt(var + LN_EPS) * g + b


def _rowwise(body, row_args, const_args, outs, *, tm, name, scratch=()):
    T = row_args[0].shape[0]
    tm = min(tm, T)
    assert T % tm == 0
    in_specs = [pl.BlockSpec((tm, a.shape[1]), lambda i: (i, 0)) for a in row_args]
    in_specs += [pl.BlockSpec(a.shape, lambda i, nd=a.ndim: (0,) * nd) for a in const_args]
    out_shape = [jax.ShapeDtypeStruct((T, w), dt) for (w, dt) in outs]
    out_specs = [pl.BlockSpec((tm, w), lambda i: (i, 0)) for (w, dt) in outs]
    res = pl.pallas_call(
        body, out_shape=out_shape, grid=(T // tm,), in_specs=in_specs, out_specs=out_specs,
        scratch_shapes=list(scratch), compiler_params=_cparams(1), name=name,
    )(*row_args, *const_args)
    return res


def _mix_proj_body(x_ref, xp_ref, mu_ref, w_ref, o_ref):
    x = x_ref[...]
    xs = x + (xp_ref[...] - x) * mu_ref[...]
    o_ref[...] = _dot(xs.astype(BF16), w_ref[...])


def _lora_body(has_v, x_ref, xp_ref, mu_ref, w0_ref, wa_ref, wb_ref, a0_ref, aa_ref, ab_ref,
               ga_ref, gb_ref, *rest):
    if has_v:
        v0_ref, va_ref, vb_ref, logw_ref, a_ref, g_ref, vg_ref = rest
    else:
        logw_ref, a_ref, g_ref = rest
    x = x_ref[...]
    dx = xp_ref[...] - x
    mix = lambda c: (x + dx * mu_ref[c:c + 1, :]).astype(BF16)
    w_raw = w0_ref[...] + _dot(jnp.tanh(_dot(mix(3), wa_ref[...])).astype(BF16), wb_ref[...])
    logw_ref[...] = (-0.6065306597126334) * jax.nn.sigmoid(w_raw)
    a_ref[...] = jax.nn.sigmoid(a0_ref[...] + _dot(_dot(mix(4), aa_ref[...]).astype(BF16), ab_ref[...]))
    g_ref[...] = _dot(jax.nn.sigmoid(_dot(mix(5), ga_ref[...])).astype(BF16), gb_ref[...])
    if has_v:
        vg_ref[...] = jax.nn.sigmoid(v0_ref[...] + _dot(_dot(mix(2), va_ref[...]).astype(BF16), vb_ref[...]))


def _out_proj_ln_body(alpha, y_ref, g_ref, x_ref, w_ref, lg_ref, lb_ref, o_ref):
    h = _dot((y_ref[...].astype(F32) * g_ref[...].astype(F32)).astype(BF16), w_ref[...])
    o_ref[...] = _layer_norm(alpha * x_ref[...] + h, lg_ref[...], lb_ref[...])


def _proj_rms_aug_body(scale, is_query, x_ref, c_ref, w_ref, gain_ref, o_ref):
    acc = _dot(x_ref[...].astype(BF16), w_ref[...])
    c2 = c_ref[...] * LOG2E
    lane = lax.broadcasted_iota(jnp.int32, c2.shape, 1)
    for h in range(acc.shape[1] // FOX_HEAD):
        t = acc[:, h * FOX_HEAD:(h + 1) * FOX_HEAD]
        ms = jnp.mean(t * t, axis=-1, keepdims=True)
        base = 2 * h * FOX_HEAD
        o_ref[:, base:base + FOX_HEAD] = (t * lax.rsqrt(ms + RMS_EPS) * gain_ref[...] * scale).astype(o_ref.dtype)
        ch = jnp.sum(jnp.where(lane == h, c2, 0.0), axis=-1, keepdims=True)
        if not is_query:
            ch = -ch
        hi = ch.astype(BF16).astype(F32)
        mid = (ch - hi).astype(BF16).astype(F32)
        low = ch - hi - mid
        t0 = 0 if is_query else 3
        one = ((lane >= 3 - t0) & (lane < 6 - t0)).astype(F32)
        aug = jnp.where(lane == t0, hi, jnp.where(lane == t0 + 1, mid, jnp.where(lane == t0 + 2, low, one)))
        o_ref[:, base + FOX_HEAD:base + 2 * FOX_HEAD] = aug.astype(o_ref.dtype)


def _proj_body(x_ref, w_ref, o_ref):
    o_ref[...] = _dot(x_ref[...].astype(BF16), w_ref[...]).astype(o_ref.dtype)


def _proj_sigmoid_body(x_ref, w_ref, o_ref):
    o_ref[...] = jax.nn.sigmoid(_dot(x_ref[...].astype(BF16), w_ref[...])).astype(o_ref.dtype)


def _log_sigmoid(z):
    return jnp.minimum(z, 0.0) - jnp.log(1.0 + jnp.exp(-jnp.abs(z)))


def _forget_cumsum_body(x_ref, w_ref, b_ref, c_ref, carry_ref):
    @pl.when(pl.program_id(0) == 0)
    def _():
        carry_ref[...] = jnp.zeros_like(carry_ref)

    logf = _log_sigmoid(_dot(x_ref[...].astype(BF16), w_ref[...]) + b_ref[...])
    tm = logf.shape[0]
    tri = (lax.broadcasted_iota(jnp.int32, (tm, tm), 0) >= lax.broadcasted_iota(jnp.int32, (tm, tm), 1))
    tri = tri.astype(BF16)
    hi = logf.astype(BF16)
    r1 = logf - hi.astype(F32)
    mid = r1.astype(BF16)
    lo = (r1 - mid.astype(F32)).astype(BF16)
    c = _dot(tri, hi) + _dot(tri, mid) + _dot(tri, lo) + carry_ref[...]
    c_ref[...] = c
    carry_ref[...] = c[tm - 1:tm, :]


def _half_sum(x, lo):
    s_lo = jnp.sum(jnp.where(lo, x, 0.0), axis=-1, keepdims=True)
    s_hi = jnp.sum(jnp.where(lo, 0.0, x), axis=-1, keepdims=True)
    return jnp.where(lo, s_lo, s_hi)


def _to_dd(x, lo, roll):
    xr = roll(x, RWKV_HEAD, 1)
    return jnp.concatenate([jnp.where(lo, x, xr), jnp.where(lo, xr, x)], axis=0)


def _interleave(gens):
    gens = list(gens)
    while gens:
        alive = []
        for gen in gens:
            try:
                next(gen)
                alive.append(gen)
            except StopIteration:
                pass
        gens = alive


def _chunk_phase1(r, logw, k_raw, v, a, k_k, k_a, r_k, roll, emit):
    C = r.shape[0]
    lane = lax.broadcasted_iota(jnp.int32, (C, LANES), 1)
    lo = lane < RWKV_HEAD
    kk = k_raw * k_k
    kk = kk / jnp.maximum(jnp.sqrt(_half_sum(kk * kk, lo)), 1e-12)
    k = k_raw * (1.0 + (a - 1.0) * k_a)
    bonus = _half_sum(r * k * r_k, lo) * v
    b = kk * a
    row = lax.broadcasted_iota(jnp.int32, (C, LANES), 0)
    cw = logw
    s = 1
    while s < C:
        cw = cw + jnp.where(row >= s, roll(cw, s, 0), 0.0)
        s *= 2
    cw_end = cw[C - 1:C, :]
    g_in = jnp.exp(cw)
    g_ex = jnp.exp(cw - logw)
    g_inv = jnp.exp(-cw)
    g_rem = jnp.exp(cw_end - cw)
    dd = lambda x: _to_dd(x, lo, roll)
    a_t = dd(-kk * g_ex)
    r_t = dd(r * g_in)
    k_h = dd(k * g_inv)
    b_h = dd(b * g_inv)
    k_g = dd(k * g_rem)
    b_g = dd(b * g_rem)
    v_d = dd(v)
    ri = lax.broadcasted_iota(jnp.int32, (2 * C, 2 * C), 0)
    ci = lax.broadcasted_iota(jnp.int32, (2 * C, 2 * C), 1)
    same = (ri < C) == (ci < C)
    strict = same & (ri > ci)
    incl = same & (ri >= ci)
    ar = jnp.concatenate([a_t, r_t], axis=0).astype(BF16)
    kb = jnp.concatenate([k_h, b_h], axis=0).astype(BF16)
    yield
    prod = 0.5 * _dot_nt(ar, kb)
    yield
    a_ak = jnp.where(strict, prod[:2 * C, :2 * C], 0.0)
    a_ab = jnp.where(strict, prod[:2 * C, 2 * C:], 0.0)
    a_rk = jnp.where(incl, prod[2 * C:, :2 * C], 0.0)
    a_rb = jnp.where(incl, prod[2 * C:, 2 * C:], 0.0)
    v16 = v_d.astype(BF16)
    n16 = a_ab.astype(BF16)
    npow = _dot(n16, n16)
    akv = _dot(jnp.concatenate([a_ak, a_rk], axis=0).astype(BF16), v16)
    yield
    tinv = (ri == ci).astype(F32) + a_ab
    s = 4
    while s < C:
        both = _dot(npow.astype(BF16), jnp.concatenate([tinv, npow], axis=1).astype(BF16))
        yield
        tinv = tinv + both[:, :2 * C]
        npow = both[:, 2 * C:]
        s *= 2
    tinv = tinv + _dot(npow.astype(BF16), tinv.astype(BF16))
    yield
    pq = _dot(tinv.astype(BF16), jnp.concatenate([a_t, akv[:2 * C]], axis=1).astype(BF16))
    yield
    pq16 = pq.astype(BF16)
    rbpq = _dot(a_rb.astype(BF16), pq16)
    kbg_t = jnp.concatenate([jnp.where(same, k_g, 0.0), jnp.where(same, b_g, 0.0)], axis=0).T
    rhs = jnp.concatenate([jnp.concatenate([v16, jnp.zeros_like(v16)], axis=1),
                           jnp.concatenate([pq16[:, LANES:], pq16[:, :LANES]], axis=1)], axis=0)
    zm = _dot(kbg_t.astype(BF16), rhs)
    yield
    g = r_t + rbpq[:, :LANES]
    yi = akv[2 * C:] + rbpq[:, LANES:]
    m_bd = jnp.where(same, zm[:, LANES:], 0.0) + jnp.where(ri == ci, jnp.exp(cw_end), 0.0)
    emit(jnp.where(same, g, 0.0), m_bd, zm[:, :LANES], jnp.where(lo, yi[:C], yi[C:]), bonus)


def _rwkv_phase1_body(npairs, has_v, r_ref, k_ref, v_ref, lw_ref, a_ref, *rest):
    if has_v:
        vf_ref, vg_ref, kk_ref, ka_ref, rk_ref, g_ref, m_ref, z_ref, yi_ref, bo_ref = rest
    else:
        kk_ref, ka_ref, rk_ref, g_ref, m_ref, z_ref, yi_ref, bo_ref = rest
    roll = lambda x, s, ax: pltpu.roll(x, s, ax)

    def one_pair(p):
        sl = slice(p * LANES, (p + 1) * LANES)
        v = v_ref[:, sl]
        if has_v:
            v = v + (vf_ref[:, sl] - v) * vg_ref[:, sl]

        def emit(g_bd, m_bd, z, yi, bonus):
            g_ref[0, :, sl] = g_bd
            m_ref[0, :, sl] = m_bd
            z_ref[0, :, sl] = z
            yi_ref[:, sl] = yi
            bo_ref[:, sl] = bonus

        return _chunk_phase1(r_ref[:, sl], lw_ref[:, sl], k_ref[:, sl], v, a_ref[:, sl],
                             kk_ref[:, sl], ka_ref[:, sl], rk_ref[:, sl], roll, emit)

    _interleave(one_pair(p) for p in range(npairs))


def _rwkv_phase1(r, k, v, logw, a, v_first, vgate, k_k, k_a, r_k, *, pairs_per_step=PHASE1_PAIRS_PER_STEP):
    T, D = r.shape
    C = CHUNK
    n_chunks = T // C
    n_pairs = D // LANES
    pp = min(pairs_per_step, n_pairs)
    assert n_pairs % pp == 0 and T % C == 0
    has_v = v_first is not None
    W = pp * LANES
    row_spec = pl.BlockSpec((C, W), lambda c, j: (c, j))
    par_spec = pl.BlockSpec((1, W), lambda c, j: (0, j))
    mat_spec = pl.BlockSpec((1, 2 * C, W), lambda c, j: (c, 0, j))
    ins = [r, k, v, logw, a] + ([v_first, vgate] if has_v else []) + [k_k, k_a, r_k]
    in_specs = [row_spec] * (7 if has_v else 5) + [par_spec] * 3
    mat_shape = jax.ShapeDtypeStruct((n_chunks, 2 * C, D), F32)
    row_shape = jax.ShapeDtypeStruct((T, D), F32)
    out_shape = [mat_shape, mat_shape, mat_shape, row_shape, row_shape]
    out_specs = [mat_spec, mat_spec, mat_spec, row_spec, row_spec]
    return pl.pallas_call(
        functools.partial(_rwkv_phase1_body, pp, has_v),
        out_shape=out_shape, grid=(n_chunks, n_pairs // pp), in_specs=in_specs, out_specs=out_specs,
        compiler_params=_cparams(2), name="rwkv_phase1",
    )(*ins)


def _rwkv_phase2_body(npairs, g_ref, m_ref, z_ref, yi_ref, bo_ref, lg_ref, lb_ref, y_ref, s_ref):
    @pl.when(pl.program_id(0) == 0)
    def _():
        s_ref[...] = jnp.zeros_like(s_ref)

    C = CHUNK
    lane = lax.broadcasted_iota(jnp.int32, (C, LANES), 1)
    lo = lane < RWKV_HEAD

    def one_pair(p):
        sl = slice(p * LANES, (p + 1) * LANES)
        s0 = s_ref[p]
        gm = jnp.concatenate([g_ref[0, :, sl], m_ref[0, :, sl]], axis=0)
        g_hi, g_lo = _split(gm)
        s_hi, s_lo = _split(s0)
        yield
        res = _dot(g_hi, s_hi) + _dot(g_hi, s_lo) + _dot(g_lo, s_hi)
        yield
        s_ref[p] = res[2 * C:] + z_ref[0, :, sl]
        yd = res[:2 * C]
        y = jnp.where(lo, yd[:C], yd[C:]) + yi_ref[:, sl]
        mean = _half_sum(y, lo) * (1.0 / RWKV_HEAD)
        yc = y - mean
        var = _half_sum(yc * yc, lo) * (1.0 / RWKV_HEAD)
        y_ref[:, sl] = yc * lax.rsqrt(var + GN_EPS) * lg_ref[:, sl] + lb_ref[:, sl] + bo_ref[:, sl]

    _interleave(one_pair(p) for p in range(npairs))


def _rwkv_phase2(g_bd, m_bd, z, yi, bonus, lnx_g, lnx_b):
    n_chunks, C2, D = g_bd.shape
    C = CHUNK
    T = n_chunks * C
    n_pairs = D // LANES
    row_spec = pl.BlockSpec((C, D), lambda c: (c, 0))
    par_spec = pl.BlockSpec((1, D), lambda c: (0, 0))
    mat_spec = pl.BlockSpec((1, 2 * C, D), lambda c: (c, 0, 0))
    return pl.pallas_call(
        functools.partial(_rwkv_phase2_body, n_pairs),
        out_shape=jax.ShapeDtypeStruct((T, D), F32),
        grid=(n_chunks,),
        in_specs=[mat_spec, mat_spec, mat_spec, row_spec, row_spec, par_spec, par_spec],
        out_specs=row_spec,
        scratch_shapes=[pltpu.VMEM((n_pairs, 2 * C, LANES), F32)],
        compiler_params=_cparams(1), name="rwkv_phase2",
    )(g_bd, m_bd, z, yi, bonus, lnx_g, lnx_b)


def _rwkv_layer(x, v_first, vres, mu, w_rkv, w0, w_a, w_b, a0, a_a, a_b, g_a, g_b, k_k, k_a, r_k,
                lnx_g, lnx_b, w_o, ln_g, ln_b, alpha):
    T, D = x.shape
    xp = jnp.concatenate([jnp.zeros((1, D), x.dtype), x[:-1]], axis=0)
    row = lambda t: t.reshape(1, -1)
    padc = lambda w: jnp.pad(w, ((0, 0), (0, (-w.shape[1]) % LANES))).astype(BF16)
    padr = lambda w: jnp.pad(w, ((0, (-w.shape[0]) % LANES), (0, 0))).astype(BF16)
    rkv = [
        _rowwise(_mix_proj_body, [x, xp], [row(mu[c]), w_rkv[c].astype(BF16)], [(D, F32)],
                 tm=256, name=f"rwkv_proj{c}")[0]
        for c in range(3)
    ]
    consts = [mu, row(w0), padc(w_a), padr(w_b), row(a0), padc(a_a), padr(a_b), padc(g_a), padr(g_b)]
    outs = [(D, F32)] * 3
    has_v = vres is not None
    if has_v:
        v0, v_a, v_b = vres
        consts += [row(v0), padc(v_a), padr(v_b)]
        outs = outs + [(D, F32)]
    lora = _rowwise(functools.partial(_lora_body, has_v), [x, xp], consts, outs, tm=256, name="rwkv_lora")
    logw, a, g = lora[:3]
    vgate = lora[3] if has_v else None
    ph1 = _rwkv_phase1(rkv[0], rkv[1], rkv[2], logw, a, v_first if has_v else None, vgate,
                       row(k_k), row(k_a), r_k.reshape(1, D))
    g_bd, m_bd, z, yi, bonus = ph1[:5]
    if not has_v:
        v_first = rkv[2]
    y = _rwkv_phase2(g_bd, m_bd, z, yi, bonus, row(lnx_g), row(lnx_b))
    x_new = _rowwise(functools.partial(_out_proj_ln_body, alpha), [y, g, x],
                     [w_o.astype(BF16), row(ln_g), row(ln_b)], [(D, F32)], tm=256, name="rwkv_out")[0]
    return x_new, v_first


def _fox_flash_body(tq, q_ref, k_ref, v_ref, o_ref, m_ref, l_ref, acc_ref):
    i = pl.program_id(1)
    m_ref[...] = jnp.full_like(m_ref, -1e30)
    l_ref[...] = jnp.zeros_like(l_ref)
    acc_ref[...] = jnp.zeros_like(acc_ref)

    def block(j, diagonal):
        start = pl.multiple_of(j * tq, tq)
        s = _dot(q_ref[...], k_ref[0, j])
        if diagonal:
            row = lax.broadcasted_iota(jnp.int32, (tq, tq), 0)
            col = lax.broadcasted_iota(jnp.int32, (tq, tq), 1)
            s = jnp.where(col <= row, s, -1e30)
        chunks = [s[:, c * LANES:(c + 1) * LANES] for c in range(tq // LANES)]
        m_loc = functools.reduce(jnp.maximum, chunks)
        m_old = m_ref[...]
        m_new = jnp.maximum(m_old, jnp.max(m_loc, axis=-1, keepdims=True))
        alpha = jnp.exp2(m_old - m_new)
        ps = [jnp.exp2(c - m_new) for c in chunks]
        l_ref[...] = alpha * l_ref[...] + functools.reduce(jnp.add, ps)
        p = jnp.concatenate(ps, axis=1).astype(BF16)
        acc_ref[...] = alpha * acc_ref[...] + _dot(p, v_ref[pl.ds(start, tq), :])
        m_ref[...] = m_new

    def pair(jj, carry):
        block(2 * jj, False)
        block(2 * jj + 1, False)
        return carry

    lax.fori_loop(0, i // 2, pair, 0)

    @pl.when(i % 2 == 1)
    def _():
        block(i - 1, False)

    block(i, True)
    o_ref[...] = (acc_ref[...] / jnp.sum(l_ref[...], axis=-1, keepdims=True)).astype(o_ref.dtype)


def _fox_flash(q_aug, k_aug, v, *, tq=1024):
    T, D = v.shape
    H = D // FOX_HEAD
    tq = min(tq, T)
    k_t = k_aug.reshape(T // tq, tq, H, 2 * FOX_HEAD).transpose(2, 0, 3, 1)
    return pl.pallas_call(
        functools.partial(_fox_flash_body, tq),
        out_shape=jax.ShapeDtypeStruct((T, D), F32),
        grid=(H, T // tq),
        in_specs=[
            pl.BlockSpec((tq, 2 * FOX_HEAD), lambda h, i: (i, h)),
            pl.BlockSpec((1, T // tq, 2 * FOX_HEAD, tq), lambda h, i: (h, 0, 0, 0)),
            pl.BlockSpec((T, FOX_HEAD), lambda h, i: (0, h)),
        ],
        out_specs=pl.BlockSpec((tq, FOX_HEAD), lambda h, i: (i, h)),
        scratch_shapes=[pltpu.VMEM((tq, LANES), F32), pltpu.VMEM((tq, LANES), F32),
                        pltpu.VMEM((tq, FOX_HEAD), F32)],
        compiler_params=_cparams(2), name="fox_flash",
    )(q_aug, k_t, v)


def _fox_shared_kv(x, w_kvf, b_f, k_norm):
    T, D = x.shape
    H = D // FOX_HEAD
    w_f = jnp.pad(w_kvf[:, 2 * D:], ((0, 0), (0, LANES - H))).astype(BF16)
    b_pad = jnp.pad(b_f, (0, LANES - H)).reshape(1, LANES)
    c = _rowwise(_forget_cumsum_body, [x], [w_f, b_pad], [(LANES, F32)], tm=256, name="fox_forget",
                 scratch=[pltpu.VMEM((1, LANES), F32)])[0]
    k_aug = _rowwise(functools.partial(_proj_rms_aug_body, 1.0, False), [x, c],
                     [w_kvf[:, :D].astype(BF16), k_norm.reshape(1, FOX_HEAD)], [(2 * D, BF16)], tm=256,
                     name="fox_k")[0]
    v = _rowwise(_proj_body, [x], [w_kvf[:, D:2 * D].astype(BF16)], [(D, BF16)], tm=256, name="fox_v")[0]
    return k_aug, v, c


def _fox_layer(x, k_aug, v, c, w_qg, q_norm, w_o, ln_g, ln_b, alpha):
    T, D = x.shape
    row = lambda t: t.reshape(1, -1)
    q_aug = _rowwise(functools.partial(_proj_rms_aug_body, FOX_HEAD ** -0.5 * LOG2E, True), [x, c],
                     [w_qg[:, :D].astype(BF16), q_norm.reshape(1, FOX_HEAD)], [(2 * D, BF16)], tm=256,
                     name="fox_q")[0]
    gate = _rowwise(_proj_sigmoid_body, [x], [w_qg[:, D:].astype(BF16)], [(D, F32)], tm=256, name="fox_gate")[0]
    o = _fox_flash(q_aug, k_aug, v)
    return _rowwise(functools.partial(_out_proj_ln_body, alpha), [o, gate, x],
                    [w_o.astype(BF16), row(ln_g), row(ln_b)], [(D, F32)], tm=256, name="fox_out")[0]


def _router_body(x_ref, w_ref, b_ref, o_ref, cnt_ref, cnt_acc):
    @pl.when(pl.program_id(0) == 0)
    def _():
        cnt_acc[...] = jnp.zeros_like(cnt_acc)

    logits = _dot3(x_ref[...], w_ref[...])
    scores = jax.nn.sigmoid(logits)
    biased = scores + b_ref[...]
    sc = [scores[:, e:e + 1] for e in range(N_EXPERTS)]
    bi = [biased[:, e:e + 1] for e in range(N_EXPERTS)]
    gs = []
    for g in range(N_GROUPS):
        a, b, c, d = bi[4 * g:4 * g + 4]
        hi1, lo1 = jnp.maximum(a, b), jnp.minimum(a, b)
        hi2, lo2 = jnp.maximum(c, d), jnp.minimum(c, d)
        gs.append(jnp.maximum(hi1, hi2) + jnp.maximum(jnp.minimum(hi1, hi2), jnp.maximum(lo1, lo2)))
    best, g_sel = gs[0], jnp.zeros_like(gs[0], dtype=jnp.int32)
    for g in range(1, N_GROUPS):
        better = gs[g] > best
        best = jnp.where(better, gs[g], best)
        g_sel = jnp.where(better, g, g_sel)

    def pick(cols, j):
        out = cols[j]
        for g in range(1, N_GROUPS):
            out = jnp.where(g_sel == g, cols[4 * g + j], out)
        return out

    ib = [pick(bi, j) for j in range(EXPERTS_PER_GROUP)]
    isc = [pick(sc, j) for j in range(EXPERTS_PER_GROUP)]

    def argmax4(vals):
        bv, bi_, bs = vals[0], jnp.zeros_like(g_sel), isc[0]
        for j in range(1, EXPERTS_PER_GROUP):
            better = vals[j] > bv
            bv = jnp.where(better, vals[j], bv)
            bi_ = jnp.where(better, j, bi_)
            bs = jnp.where(better, isc[j], bs)
        return bi_, bs

    i1, s1 = argmax4(ib)
    ib2 = [jnp.where(i1 == j, -jnp.inf, ib[j]) for j in range(EXPERTS_PER_GROUP)]
    i2, s2 = argmax4(ib2)
    den = s1 + s2
    e1 = g_sel * EXPERTS_PER_GROUP + i1
    e2 = g_sel * EXPERTS_PER_GROUP + i2
    tm = logits.shape[0]
    lane = lax.broadcasted_iota(jnp.int32, (tm, LANES), 1)
    oh1 = lane == e1
    oh2 = lane == e2
    oh = oh1 | oh2
    before = lax.broadcasted_iota(jnp.int32, (tm, tm), 0) > lax.broadcasted_iota(jnp.int32, (tm, tm), 1)
    prefix = _dot(before.astype(BF16), oh.astype(BF16)) + cnt_acc[...]
    rank1 = jnp.sum(jnp.where(oh1, prefix, 0.0), axis=-1, keepdims=True)
    rank2 = jnp.sum(jnp.where(oh2, prefix, 0.0), axis=-1, keepdims=True)
    cnt_acc[...] = cnt_acc[...] + jnp.sum(oh.astype(F32), axis=0, keepdims=True)
    cnt_ref[...] = cnt_acc[...]
    cols = [e1.astype(F32), e2.astype(F32), s1 / den, s2 / den, rank1, rank2]
    out = jnp.zeros((tm, LANES), F32)
    for c, col in enumerate(cols):
        out = jnp.where(lane == c, col, out)
    o_ref[...] = out


def _moe_dispatch_body(tm, dest_ref, x_ref, xs_in, xs_out, sem):
    del xs_in
    i = pl.program_id(0)

    def row_copy(r, j):
        d = dest_ref[2 * (i * tm + r) + j]
        return pltpu.make_async_copy(x_ref.at[pl.ds(r, 1), :], xs_out.at[pl.ds(d, 1), :], sem.at[j])

    def issue(r, carry):
        row_copy(r, 0).start()
        row_copy(r, 1).start()
        return carry

    lax.fori_loop(0, tm, issue, 0)

    def drain(r, carry):
        row_copy(r, 0).wait()
        row_copy(r, 1).wait()
        return carry

    lax.fori_loop(0, tm, drain, 0)


def _moe_expert_body(tile_e_ref, nt_ref, xs_ref, win_ref, wout_ref, o_ref):
    i = pl.program_id(0)

    @pl.when(i < nt_ref[0])
    def _():
        F = wout_ref.shape[2]
        h = _dot(xs_ref[...].astype(BF16), win_ref[0, 0])
        act = h[:, :F] * jax.nn.silu(h[:, F:])
        o_ref[...] = _dot(act.astype(BF16), wout_ref[0, 0])

    @pl.when(i >= nt_ref[0])
    def _():
        o_ref[...] = jnp.zeros_like(o_ref)


def _moe_combine_body(tm, alpha, pos_ref, y_hbm, x_ref, route_ref, lg_ref, lb_ref, o_ref, buf, sem):
    i = pl.program_id(0)

    def row_copy(r, j):
        p = pos_ref[2 * (i * tm + r) + j]
        return pltpu.make_async_copy(y_hbm.at[pl.ds(p, 1), :], buf.at[j, pl.ds(r, 1), :], sem.at[j])

    def issue(r, carry):
        row_copy(r, 0).start()
        row_copy(r, 1).start()
        return carry

    lax.fori_loop(0, tm, issue, 0)

    def drain(r, carry):
        row_copy(r, 0).wait()
        row_copy(r, 1).wait()
        return carry

    lax.fori_loop(0, tm, drain, 0)
    route = route_ref[...]
    z = alpha * x_ref[...] + (route[:, 2:3] * buf[0] + route[:, 3:4] * buf[1])
    o_ref[...] = _layer_norm(z, lg_ref[...], lb_ref[...])


def _moe_layer(x, router_w, router_b, w_in16, w_out16, layer, ln_g, ln_b, alpha, xs_init=None, *, tm=256):
    T, D = x.shape
    E, F = w_out16.shape[1], w_out16.shape[2]
    tm = min(tm, T)
    rw_pad = jnp.pad(router_w, ((0, 0), (0, LANES - E)))
    rb_pad = jnp.pad(router_b, (0, LANES - E)).reshape(1, LANES)
    route, cnt = pl.pallas_call(
        _router_body,
        out_shape=[jax.ShapeDtypeStruct((T, LANES), F32), jax.ShapeDtypeStruct((1, LANES), F32)],
        grid=(T // tm,),
        in_specs=[pl.BlockSpec((tm, D), lambda i: (i, 0)), pl.BlockSpec((D, LANES), lambda i: (0, 0)),
                  pl.BlockSpec((1, LANES), lambda i: (0, 0))],
        out_specs=[pl.BlockSpec((tm, LANES), lambda i: (i, 0)), pl.BlockSpec((1, LANES), lambda i: (0, 0))],
        scratch_shapes=[pltpu.VMEM((1, LANES), F32)],
        compiler_params=_cparams(1), name="moe_router",
    )(x, rw_pad, rb_pad)
    counts = cnt[0, :E].astype(jnp.int32)
    padded = ((counts + tm - 1) // tm) * tm
    ends = jnp.cumsum(padded)
    starts = ends - padded
    sel = route[:, 0:2].astype(jnp.int32)
    dest = (starts[sel] + route[:, 4:6].astype(jnp.int32)).reshape(-1)
    n_rows = 2 * T + E * tm
    n_tiles = n_rows // tm
    used_tiles = (ends[-1] // tm).astype(jnp.int32).reshape(1)
    tile_start = jnp.minimum(jnp.arange(n_tiles, dtype=jnp.int32), used_tiles - 1) * tm
    tile_e = jnp.minimum(jnp.sum(ends[None, :] <= tile_start[:, None], axis=1), E - 1).astype(jnp.int32)

    x_sorted = pl.pallas_call(
        functools.partial(_moe_dispatch_body, tm),
        out_shape=jax.ShapeDtypeStruct((n_rows, D), F32),
        grid_spec=pltpu.PrefetchScalarGridSpec(
            num_scalar_prefetch=1, grid=(T // tm,),
            in_specs=[pl.BlockSpec((tm, D), lambda i, dst: (i, 0)), pl.BlockSpec(memory_space=pl.ANY)],
            out_specs=pl.BlockSpec(memory_space=pl.ANY),
            scratch_shapes=[pltpu.SemaphoreType.DMA((2,))],
        ),
        input_output_aliases={2: 0},
        compiler_params=_cparams(1), name="moe_dispatch",
    )(dest, x, jnp.zeros((n_rows, D), F32) if xs_init is None else xs_init)

    live = lambda i, nt: jnp.minimum(i, nt[0] - 1)
    y_sorted = pl.pallas_call(
        _moe_expert_body,
        out_shape=jax.ShapeDtypeStruct((n_rows, D), F32),
        grid_spec=pltpu.PrefetchScalarGridSpec(
            num_scalar_prefetch=2, grid=(n_tiles,),
            in_specs=[
                pl.BlockSpec((tm, D), lambda i, te, nt: (live(i, nt), 0)),
                pl.BlockSpec((1, 1, D, 2 * F), lambda i, te, nt: (layer, te[i], 0, 0)),
                pl.BlockSpec((1, 1, F, D), lambda i, te, nt: (layer, te[i], 0, 0)),
            ],
            out_specs=pl.BlockSpec((tm, D), lambda i, te, nt: (i, 0)),
        ),
        compiler_params=_cparams(1), name="moe_experts",
    )(tile_e, used_tiles, x_sorted, w_in16, w_out16)

    row = lambda t: t.reshape(1, -1)
    out = pl.pallas_call(
        functools.partial(_moe_combine_body, tm, alpha),
        out_shape=jax.ShapeDtypeStruct((T, D), F32),
        grid_spec=pltpu.PrefetchScalarGridSpec(
            num_scalar_prefetch=1, grid=(T // tm,),
            in_specs=[
                pl.BlockSpec(memory_space=pl.ANY),
                pl.BlockSpec((tm, D), lambda i, pos: (i, 0)),
                pl.BlockSpec((tm, LANES), lambda i, pos: (i, 0)),
                pl.BlockSpec((1, D), lambda i, pos: (0, 0)),
                pl.BlockSpec((1, D), lambda i, pos: (0, 0)),
            ],
            out_specs=pl.BlockSpec((tm, D), lambda i, pos: (i, 0)),
            scratch_shapes=[pltpu.VMEM((2, tm, D), F32), pltpu.SemaphoreType.DMA((2,))],
        ),
        compiler_params=_cparams(1), name="moe_combine",
    )(dest, y_sorted, x, route, row(ln_g), row(ln_b))
    return out, x_sorted


def kernel(x, rw_mu, rw_w_rkv, rw_w0, rw_w_a, rw_w_b, rw_a0, rw_a_a, rw_a_b, rw_v0, rw_v_a, rw_v_b, rw_g_a, rw_g_b, rw_k_k, rw_k_a, rw_r_k, rw_lnx_g, rw_lnx_b, rw_w_o, fx_w_kvf, fx_b_f, fx_k_norm, fx_w_qg, fx_q_norm, fx_w_o, router_w, router_b, moe_w_in, moe_w_out, ln_g, ln_b):
    B, T, D = x.shape
    depth = ln_g.shape[0]
    n_a = rw_mu.shape[0]
    alpha = (2 * depth) ** 0.25
    w_in16 = moe_w_in.astype(BF16)
    w_out16 = moe_w_out.astype(BF16)
    outs = []
    for bi in range(B):
        h = x[bi]
        v_first = None
        kv = None
        xs_buf = None
        for l in range(depth):
            if l < n_a:
                vres = None if l == 0 else (rw_v0[l - 1], rw_v_a[l - 1], rw_v_b[l - 1])
                h, v_first = _rwkv_layer(
                    h, v_first, vres, rw_mu[l], rw_w_rkv[l], rw_w0[l], rw_w_a[l], rw_w_b[l], rw_a0[l],
                    rw_a_a[l], rw_a_b[l], rw_g_a[l], rw_g_b[l], rw_k_k[l], rw_k_a[l], rw_r_k[l],
                    rw_lnx_g[l], rw_lnx_b[l], rw_w_o[l], ln_g[l, 0], ln_b[l, 0], alpha)
            else:
                if kv is None:
                    kv = _fox_shared_kv(h, fx_w_kvf, fx_b_f, fx_k_norm)
                j = l - n_a
                h = _fox_layer(h, *kv, fx_w_qg[j], fx_q_norm[j], fx_w_o[j], ln_g[l, 0], ln_b[l, 0], alpha)
            h, xs_buf = _moe_layer(h, router_w, router_b, w_in16, w_out16, l, ln_g[l, 1], ln_b[l, 1], alpha, xs_buf)
        outs.append(h)
    return jnp.stack(outs, axis=0)
```
